```python
import jax, jax.numpy as jnp
from jax import lax
import numpy as np

D_MODEL = 4096
BATCH = 4
SEQ = 2048
DEPTH = 2
DEC_BATCH = 32
DEC_SEQ = 16
PAST_LEN = 4096

CHUNK = 64
N_MIXERS = 2
N_MLA_LAYERS = (DEPTH + 1) // 2
N_POOL_LAYERS = DEPTH // 2
N_HEADS = 32
Q_LORA = 1024
KV_LORA = 512
QK_NOPE = 128
QK_ROPE = 64
QK_HEAD = QK_NOPE + QK_ROPE
V_HEAD = 128
ROPE_THETA = 10000.0
ATTN_SCALE = QK_HEAD ** -0.5
Q_BLOCK = 128
NEG_INF = -1e30
POOL_WINDOWS = (2, 4, 8, 16)
N_POOL_GROUPS = len(POOL_WINDOWS)
POOL_GROUP_WIDTH = D_MODEL // N_POOL_GROUPS
POOL_HIST = max(POOL_WINDOWS) - 1
D_FF = 11008
CONV_W = 3
RMS_EPS = 1e-6

kernel_name = "streaming_mla_pool_convffn_adaln_step"


def rms_norm(x, w):
    xf = x.astype(jnp.float32)
    y = xf * lax.rsqrt(jnp.mean(xf * xf, axis=-1, keepdims=True) + RMS_EPS)
    return (y * w.astype(jnp.float32)).astype(x.dtype)


def ada_params(c, w, b):
    m = (jax.nn.silu(c) @ w + b).reshape(c.shape[0], 6, 1, D_MODEL)
    return [m[:, k] for k in range(6)]


def modulate(h, shift, scale):
    return h * (1 + scale) + shift


def rope(x, pos):
    half = QK_ROPE // 2
    inv = jnp.power(ROPE_THETA, -jnp.arange(half, dtype=jnp.float32) / half)
    ang = pos.astype(jnp.float32)[:, None] * inv[None, :]
    cos = jnp.cos(ang)[None, :, None, :]
    sin = jnp.sin(ang)[None, :, None, :]
    xf = x.astype(jnp.float32)
    x1, x2 = xf[..., :half], xf[..., half:]
    return jnp.concatenate([x1 * cos - x2 * sin, x1 * sin + x2 * cos], axis=-1).astype(x.dtype)


def mla_project(u, pos, w_dq, q_norm_w, w_uq, w_dkv, kv_norm_w):
    cq = rms_norm(u @ w_dq, q_norm_w)
    q = jnp.einsum('bsc,chd->bshd', cq, w_uq)
    q_nope = q[..., :QK_NOPE]
    q_pe = rope(q[..., QK_NOPE:], pos)
    kv_a = u @ w_dkv
    ckv = rms_norm(kv_a[..., :KV_LORA], kv_norm_w)
    kpe = rope(kv_a[..., None, KV_LORA:], pos)[:, :, 0]
    return q_nope, q_pe, ckv, kpe


def mla_attend_prompt(q_nope, q_pe, ckv, kpe, w_ukv):
    B, S = ckv.shape[:2]
    k_nope = jnp.einsum('bsc,chd->bshd', ckv, w_ukv[..., :QK_NOPE])
    v = jnp.einsum('bsc,chd->bshd', ckv, w_ukv[..., QK_NOPE:])
    n_blk = S // Q_BLOCK
    kchunk = jnp.arange(S) // CHUNK

    def blocks(t):
        return t.reshape(B, n_blk, Q_BLOCK, *t.shape[2:]).swapaxes(0, 1)

    def attend(args):
        qn, qp, start = args
        s = (jnp.einsum('bqhd,bkhd->bhqk', qn, k_nope, preferred_element_type=jnp.float32)
             + jnp.einsum('bqhr,bkr->bhqk', qp, kpe, preferred_element_type=jnp.float32)) * ATTN_SCALE
        qchunk = (start + jnp.arange(Q_BLOCK)) // CHUNK
        s = jnp.where(kchunk[None, :] <= qchunk[:, None], s, NEG_INF)
        p = jax.nn.softmax(s, axis=-1).astype(v.dtype)
        return jnp.einsum('bhqk,bkhd->bqhd', p, v)

    o = lax.map(attend, (blocks(q_nope), blocks(q_pe), jnp.arange(n_blk) * Q_BLOCK))
    return o.swapaxes(0, 1).reshape(B, S, N_HEADS * V_HEAD)


def mla_attend_sample(q_nope, q_pe, ckv_all, kpe_all, qpos, w_ukv):
    B, DS = q_nope.shape[:2]
    q_lat = jnp.einsum('bqhd,chd->bqhc', q_nope, w_ukv[..., :QK_NOPE])
    s = (jnp.einsum('bqhc,bkc->bhqk', q_lat, ckv_all, preferred_element_type=jnp.float32)
         + jnp.einsum('bqhr,bkr->bhqk', q_pe, kpe_all, preferred_element_type=jnp.float32)) * ATTN_SCALE
    kchunk = jnp.arange(ckv_all.shape[1]) // CHUNK
    s = jnp.where(kchunk[None, :] <= (qpos // CHUNK)[:, None], s, NEG_INF)
    p = jax.nn.softmax(s, axis=-1).astype(ckv_all.dtype)
    o_lat = jnp.einsum('bhqk,bkc->bqhc', p, ckv_all)
    o = jnp.einsum('bqhc,chd->bqhd', o_lat, w_ukv[..., QK_NOPE:])
    return o.reshape(B, DS, N_HEADS * V_HEAD)


def pool_mix(u, prev, pos, w_pool, pool_scale):
    B, S, _ = u.shape
    xs = jnp.concatenate([prev.astype(u.dtype), u], axis=1)
    cs = jnp.concatenate([jnp.zeros((B, 1, D_MODEL), jnp.float32),
                          jnp.cumsum(xs.astype(jnp.float32), axis=1)], axis=1)
    upper = cs[:, POOL_HIST + 1:]
    uf = u.astype(jnp.float32)
    diffs = []
    for g, w in enumerate(POOL_WINDOWS):
        sl = slice(g * POOL_GROUP_WIDTH, (g + 1) * POOL_GROUP_WIDTH)
        lower = cs[:, POOL_HIST + 1 - w:POOL_HIST + 1 - w + S, sl]
        cnt = jnp.minimum(pos + 1, w).astype(jnp.float32)[None, :, None]
        diffs.append((upper[..., sl] - lower) / cnt - uf[..., sl])
    d = jnp.stack(diffs, axis=2).astype(u.dtype)
    y = jnp.einsum('bsgi,gio->bsgo', d, w_pool).reshape(B, S, D_MODEL)
    return y * pool_scale, xs[:, -POOL_HIST:]


def conv_ffn(u, prev, w_up, conv_w, conv_b, w_down):
    h = u @ w_up
    S = h.shape[1]
    hp = jnp.concatenate([prev.astype(h.dtype), h], axis=1)
    hc = conv_b + hp[:, 0:S] * conv_w[0]
    for k in range(1, CONV_W):
        hc = hc + hp[:, k:k + S] * conv_w[k]
    g, v = jnp.split(hc, 2, axis=-1)
    return (jax.nn.silu(g) * v) @ w_down, hp[:, -(CONV_W - 1):]


def setup_inputs(seed: int = 0) -> dict:
    key = jax.random.key(seed)
    ks = iter(jax.random.split(key, 32))

    def nrm(shape, scale):
        return jax.random.normal(next(ks), shape, jnp.float32) * scale

    return {
        "x_prompt": nrm((BATCH, SEQ, D_MODEL), 1.0),
        "x_sample": nrm((DEC_BATCH, DEC_SEQ, D_MODEL), 1.0),
        "c_prompt": nrm((BATCH, D_MODEL), 1.0),
        "c_sample": nrm((DEC_BATCH, D_MODEL), 1.0),
        "cache_ckv": nrm((N_MLA_LAYERS, DEC_BATCH, PAST_LEN, KV_LORA), 1.0),
        "cache_kpe": nrm((N_MLA_LAYERS, DEC_BATCH, PAST_LEN, QK_ROPE), 1.0),
        "state_pool": nrm((N_POOL_LAYERS, DEC_BATCH, POOL_HIST, D_MODEL), 1.0),
        "state_conv": nrm((DEPTH, DEC_BATCH, CONV_W - 1, 2 * D_FF), 1.0),
        "w_ada": nrm((DEPTH, D_MODEL, 6 * D_MODEL), 0.5 * D_MODEL ** -0.5),
        "b_ada": nrm((DEPTH, 6 * D_MODEL), 0.01),
        "norm_mix_w": 1.0 + nrm((DEPTH, D_MODEL), 0.02),
        "norm_ffn_w": 1.0 + nrm((DEPTH, D_MODEL), 0.02),
        "w_dq": nrm((N_MLA_LAYERS, D_MODEL, Q_LORA), D_MODEL ** -0.5),
        "q_norm_w": 1.0 + nrm((N_MLA_LAYERS, Q_LORA), 0.02),
        "w_uq": nrm((N_MLA_LAYERS, Q_LORA, N_HEADS, QK_HEAD), Q_LORA ** -0.5),
        "w_dkv": nrm((N_MLA_LAYERS, D_MODEL, KV_LORA + QK_ROPE), D_MODEL ** -0.5),
        "kv_norm_w": 1.0 + nrm((N_MLA_LAYERS, KV_LORA), 0.02),
        "w_ukv": nrm((N_MLA_LAYERS, KV_LORA, N_HEADS, QK_NOPE + V_HEAD), KV_LORA ** -0.5),
        "w_o": nrm((N_MLA_LAYERS, N_HEADS * V_HEAD, D_MODEL), (N_HEADS * V_HEAD) ** -0.5),
        "w_pool": nrm((N_POOL_LAYERS, N_POOL_GROUPS, POOL_GROUP_WIDTH, POOL_GROUP_WIDTH), POOL_GROUP_WIDTH ** -0.5),
        "pool_scale": 1.0 + nrm((N_POOL_LAYERS, D_MODEL), 0.1),
        "w_up": nrm((DEPTH, D_MODEL, 2 * D_FF), D_MODEL ** -0.5),
        "conv_w": nrm((DEPTH, CONV_W, 2 * D_FF), CONV_W ** -0.5),
        "conv_b": nrm((DEPTH, 2 * D_FF), 0.01),
        "w_down": nrm((DEPTH, D_FF, D_MODEL), D_FF ** -0.5),
        "final_norm_w": 1.0 + nrm((D_MODEL,), 0.02),
    }


def reference(x_prompt, x_sample, c_prompt, c_sample, cache_ckv, cache_kpe, state_pool, state_conv,
              w_ada, b_ada, norm_mix_w, norm_ffn_w,
              w_dq, q_norm_w, w_uq, w_dkv, kv_norm_w, w_ukv, w_o,
              w_pool, pool_scale,
              w_up, conv_w, conv_b, w_down, final_norm_w):
    xp, xs = x_prompt, x_sample
    Bp, S = xp.shape[:2]
    DS = xs.shape[1]
    P = cache_ckv.shape[2]
    pos_p = jnp.arange(S)
    pos_s = P + jnp.arange(DS)
    zero_pool = jnp.zeros((Bp, POOL_HIST, D_MODEL), xp.dtype)
    zero_conv = jnp.zeros((Bp, CONV_W - 1, 2 * D_FF), xp.dtype)
    ckv_p_l, kpe_p_l, ckv_s_l, kpe_s_l = [], [], [], []
    pool_p_l, pool_s_l, conv_p_l, conv_s_l = [], [], [], []

    for i in range(DEPTH):
        j = i // N_MIXERS
        sh_mp, sc_mp, g_mp, sh_fp, sc_fp, g_fp = ada_params(c_prompt, w_ada[i], b_ada[i])
        sh_ms, sc_ms, g_ms, sh_fs, sc_fs, g_fs = ada_params(c_sample, w_ada[i], b_ada[i])
        up = modulate(rms_norm(xp, norm_mix_w[i]), sh_mp, sc_mp)
        us = modulate(rms_norm(xs, norm_mix_w[i]), sh_ms, sc_ms)
        if i % N_MIXERS == 0:
            qn, qpe, ckv_p, kpe_p = mla_project(up, pos_p, w_dq[j], q_norm_w[j], w_uq[j], w_dkv[j], kv_norm_w[j])
            yp = mla_attend_prompt(qn, qpe, ckv_p, kpe_p, w_ukv[j]) @ w_o[j]
            qn, qpe, ckv_s, kpe_s = mla_project(us, pos_s, w_dq[j], q_norm_w[j], w_uq[j], w_dkv[j], kv_norm_w[j])
            ckv_all = jnp.concatenate([cache_ckv[j].astype(ckv_s.dtype), ckv_s], axis=1)
            kpe_all = jnp.concatenate([cache_kpe[j].astype(kpe_s.dtype), kpe_s], axis=1)
            ys = mla_attend_sample(qn, qpe, ckv_all, kpe_all, pos_s, w_ukv[j]) @ w_o[j]
            ckv_p_l.append(ckv_p)
            kpe_p_l.append(kpe_p)
            ckv_s_l.append(ckv_s)
            kpe_s_l.append(kpe_s)
        else:
            yp, pst_p = pool_mix(up, zero_pool, pos_p, w_pool[j], pool_scale[j])
            ys, pst_s = pool_mix(us, state_pool[j], pos_s, w_pool[j], pool_scale[j])
            pool_p_l.append(pst_p)
            pool_s_l.append(pst_s)
        xp = xp + g_mp * yp
        xs = xs + g_ms * ys
        hp = modulate(rms_norm(xp, norm_ffn_w[i]), sh_fp, sc_fp)
        hs = modulate(rms_norm(xs, norm_ffn_w[i]), sh_fs, sc_fs)
        fp, cst_p = conv_ffn(hp, zero_conv, w_up[i], conv_w[i], conv_b[i], w_down[i])
        fs, cst_s = conv_ffn(hs, state_conv[i], w_up[i], conv_w[i], conv_b[i], w_down[i])
        conv_p_l.append(cst_p)
        conv_s_l.append(cst_s)
        xp = xp + g_fp * fp
        xs = xs + g_fs * fs

    y_prompt = rms_norm(xp, final_norm_w)
    y_sample = rms_norm(xs, final_norm_w)
    return (y_prompt, y_sample,
            jnp.stack(ckv_p_l), jnp.stack(kpe_p_l), jnp.stack(ckv_s_l), jnp.stack(kpe_s_l),
            jnp.stack(pool_p_l), jnp.stack(pool_s_l), jnp.stack(conv_p_l), jnp.stack(conv_s_l))
```

```python
import functools

import jax
import jax.numpy as jnp
from jax import lax
from jax.experimental import pallas as pl
from jax.experimental.pallas import tpu as pltpu

F32 = jnp.float32
BF16 = jnp.bfloat16

CHUNK = 64
ROPE_THETA = 10000.0
RMS_EPS = 1e-6
NEG_INF = -1e30
POOL_WINDOWS = (2, 4, 8, 16)
POOL_HALO = 32
V7X_VMEM_CAP_MB = 56


def _cparams(sem, vmem_mb):
    return pltpu.CompilerParams(dimension_semantics=sem, vmem_limit_bytes=min(vmem_mb, V7X_VMEM_CAP_MB) * 2 ** 20)


def _pick(dim, pref, align):
    t = min(pref, dim)
    t -= t % align
    while t >= align:
        if dim % t == 0:
            return t
        t -= align
    return dim


def _silu(x):
    return x * (1.0 / (1.0 + jnp.exp(-x)))


def _rms(x, w):
    return x * lax.rsqrt(jnp.mean(x * x, axis=-1, keepdims=True) + RMS_EPS) * w


def _rope_rot(x):
    n = x.shape[-1]
    lane = lax.broadcasted_iota(jnp.int32, x.shape, x.ndim - 1)
    return jnp.where(lane % 64 < 32, -pltpu.roll(x, n - 32, x.ndim - 1), pltpu.roll(x, 32, x.ndim - 1))


def _rope(x, cos_ref, sin_ref):
    reps = x.shape[-1] // 128
    c = cos_ref[...]
    s = sin_ref[...]
    if reps > 1:
        c = jnp.concatenate([c] * reps, axis=-1)
        s = jnp.concatenate([s] * reps, axis=-1)
    return x * c + _rope_rot(x) * s


def _ada_kernel(c_ref, w_ref, b_ref, o_ref):
    a = _silu(c_ref[...]).astype(BF16)
    o_ref[...] = jnp.dot(a, w_ref[...].astype(BF16), preferred_element_type=F32) + b_ref[...]


def _ada(c, w_ada, b_ada):
    n_layers, d, n = w_ada.shape
    bc = c.shape[0]
    tn = _pick(n, 512, 128)
    return pl.pallas_call(
        _ada_kernel,
        grid=(n_layers, n // tn),
        in_specs=[pl.BlockSpec((bc, d), lambda l, j: (0, 0)),
                  pl.BlockSpec((None, d, tn), lambda l, j: (l, 0, j)),
                  pl.BlockSpec((None, 1, tn), lambda l, j: (l, 0, j))],
        out_specs=pl.BlockSpec((None, bc, tn), lambda l, j: (l, 0, j)),
        out_shape=jax.ShapeDtypeStruct((n_layers, bc, n), F32),
        compiler_params=_cparams(("parallel", "parallel"), 4 * d * tn * 3 // 2 ** 20 + 8),
        name="ada",
    )(c, w_ada, b_ada.reshape(n_layers, 1, n))


def _normmod_kernel(x_ref, w_ref, sh_ref, sc_ref, o_ref):
    y = _rms(x_ref[...], w_ref[...])
    o_ref[...] = (y * (1.0 + sc_ref[...]) + sh_ref[...]).astype(o_ref.dtype)


def _norm_kernel(x_ref, w_ref, o_ref):
    o_ref[...] = _rms(x_ref[...], w_ref[...]).astype(o_ref.dtype)


def _row_blocks(b, s):
    if s >= 256:
        return 1, _pick(s, 256, 16)
    return _pick(b, max(1, 256 // s), 1), s


def _normmod(x, w, layer, mod, k_shift, k_scale, out_dtype):
    b, s, d = x.shape
    nb, ts = _row_blocks(b, s)
    w3 = w.reshape(w.shape[0], 1, d)
    return pl.pallas_call(
        _normmod_kernel,
        grid=(b // nb, s // ts),
        in_specs=[pl.BlockSpec((nb, ts, d), lambda i, j: (i, j, 0)),
                  pl.BlockSpec((1, 1, d), lambda i, j: (layer, 0, 0)),
                  pl.BlockSpec((nb, None, 1, d), lambda i, j: (i, k_shift, 0, 0)),
                  pl.BlockSpec((nb, None, 1, d), lambda i, j: (i, k_scale, 0, 0))],
        out_specs=pl.BlockSpec((nb, ts, d), lambda i, j: (i, j, 0)),
        out_shape=jax.ShapeDtypeStruct((b, s, d), out_dtype),
        compiler_params=_cparams(("parallel", "parallel"), 32),
        name="normmod",
    )(x, w3, mod, mod)


def _final_norm(x, w):
    b, s, d = x.shape
    nb, ts = _row_blocks(b, s)
    return pl.pallas_call(
        _norm_kernel,
        grid=(b // nb, s // ts),
        in_specs=[pl.BlockSpec((nb, ts, d), lambda i, j: (i, j, 0)),
                  pl.BlockSpec((1, 1, d), lambda i, j: (0, 0, 0))],
        out_specs=pl.BlockSpec((nb, ts, d), lambda i, j: (i, j, 0)),
        out_shape=jax.ShapeDtypeStruct((b, s, d), F32),
        compiler_params=_cparams(("parallel", "parallel"), 32),
        name="final_norm",
    )(x, w.reshape(1, 1, d))


def _mm(name, grid, a, a_spec, b, b_spec, extras, outs, epilogue, *, trans_b=False, sem, vmem_mb):
    n_ex, n_out = len(extras), len(outs)
    dn = (((1,), (1 if trans_b else 0,)), ((), ()))

    def kern(*refs):
        acc = lax.dot_general(refs[0][...].astype(BF16), refs[1][...].astype(BF16), dn, preferred_element_type=F32)
        epilogue(acc, refs[2:2 + n_ex], refs[2 + n_ex:2 + n_ex + n_out])

    return pl.pallas_call(
        kern,
        grid=grid,
        in_specs=[a_spec, b_spec] + [s for _, s in extras],
        out_specs=[s for _, s in outs],
        out_shape=[sd for sd, _ in outs],
        compiler_params=_cparams(sem, vmem_mb),
        name=name,
    )(a, b, *[x for x, _ in extras])


def _rope_tables(pos):
    half = 32
    inv = jnp.power(ROPE_THETA, -jnp.arange(half, dtype=F32) / half)
    ang = pos.astype(F32)[:, None] * inv[None, :]
    return jnp.tile(jnp.cos(ang), (1, 4)), jnp.tile(jnp.sin(ang), (1, 4))


def _mla_project(u2, pos_rows, w_in, q_norm_w, kv_norm_w, layer, q_lora, kv_lora):
    m, d = u2.shape
    n = w_in.shape[1]
    period = pos_rows.shape[0]
    tm = _pick(period, 512, 16)
    cos, sin = _rope_tables(pos_rows)
    npb = period // tm

    def epilogue(acc, ex, o):
        qw_ref, kw_ref, cos_ref, sin_ref = ex
        o[0][...] = _rms(acc[:, :q_lora], qw_ref[...]).astype(BF16)
        ckv = _rms(acc[:, q_lora:q_lora + kv_lora], kw_ref[...])
        o[1][...] = ckv
        o[2][...] = ckv.astype(BF16)
        kpe = _rope(acc[:, q_lora + kv_lora:], cos_ref, sin_ref)[:, :64]
        o[3][...] = kpe
        o[4][...] = kpe.astype(BF16)

    row = lambda width: pl.BlockSpec((tm, width), lambda i: (i, 0))
    return _mm(
        "mla_in", (m // tm,), u2, pl.BlockSpec((tm, d), lambda i: (i, 0)),
        w_in, pl.BlockSpec((d, n), lambda i: (0, 0)),
        [(q_norm_w.reshape(-1, 1, q_lora), pl.BlockSpec((None, 1, q_lora), lambda i: (layer, 0, 0))),
         (kv_norm_w.reshape(-1, 1, kv_lora), pl.BlockSpec((None, 1, kv_lora), lambda i: (layer, 0, 0))),
         (cos, pl.BlockSpec((tm, 128), lambda i: (i % npb, 0))),
         (sin, pl.BlockSpec((tm, 128), lambda i: (i % npb, 0)))],
        [(jax.ShapeDtypeStruct((m, q_lora), BF16), row(q_lora)),
         (jax.ShapeDtypeStruct((m, kv_lora), F32), row(kv_lora)),
         (jax.ShapeDtypeStruct((m, kv_lora), BF16), row(kv_lora)),
         (jax.ShapeDtypeStruct((m, 64), F32), row(64)),
         (jax.ShapeDtypeStruct((m, 64), BF16), row(64))],
        epilogue, sem=("parallel",), vmem_mb=(4 * d * n + 4 * tm * d + 16 * tm * n) // 2 ** 20 + 8)


def _q_heads(cq, w_nope, w_pe, pos_rows, scale):
    m, kq = cq.shape
    period = pos_rows.shape[0]
    tm = _pick(period, 512, 16)
    npb = period // tm
    cos, sin = _rope_tables(pos_rows)

    def plain(acc, ex, o):
        o[0][...] = (acc * scale).astype(BF16)

    def roped(acc, ex, o):
        o[0][...] = (_rope(acc, ex[0], ex[1]) * scale).astype(BF16)

    outs = []
    for name, w, extras, epi in (("q_nope", w_nope, [], plain),
                                 ("q_pe", w_pe, [(cos, pl.BlockSpec((tm, 128), lambda i, j: (i % npb, 0))),
                                                 (sin, pl.BlockSpec((tm, 128), lambda i, j: (i % npb, 0)))], roped)):
        n = w.shape[1]
        tn = _pick(n, 2048, 128)
        outs.append(_mm(name, (m // tm, n // tn), cq, pl.BlockSpec((tm, kq), lambda i, j: (i, 0)),
                        w, pl.BlockSpec((kq, tn), lambda i, j: (0, j)), extras,
                        [(jax.ShapeDtypeStruct((m, n), BF16), pl.BlockSpec((tm, tn), lambda i, j: (i, j)))],
                        epi, sem=("parallel", "parallel"), vmem_mb=40)[0])
    return outs


def _kv_heads(ckv_b, w_ukv3, layer, n_heads):
    m, kc = ckv_b.shape
    tm = _pick(m, 512, 16)
    hg = 4 if n_heads % 4 == 0 else 1

    def kern(c_ref, w_ref, k_ref, v_ref):
        c = c_ref[...]
        for g in range(n_heads // hg):
            acc = jnp.dot(c, w_ref[:, g * hg * 256:(g + 1) * hg * 256].astype(BF16), preferred_element_type=F32)
            for h in range(hg):
                cols = slice((g * hg + h) * 128, (g * hg + h + 1) * 128)
                k_ref[:, cols] = acc[:, h * 256:h * 256 + 128].astype(BF16)
                v_ref[:, cols] = acc[:, h * 256 + 128:(h + 1) * 256].astype(BF16)

    out = jax.ShapeDtypeStruct((m, n_heads * 128), BF16)
    out_spec = pl.BlockSpec((tm, n_heads * 128), lambda i: (i, 0))
    return pl.pallas_call(
        kern,
        grid=(m // tm,),
        in_specs=[pl.BlockSpec((tm, kc), lambda i: (i, 0)),
                  pl.BlockSpec((None, kc, n_heads * 256), lambda i: (layer, 0, 0), pipeline_mode=pl.Buffered(1))],
        out_specs=[out_spec, out_spec],
        out_shape=[out, out],
        compiler_params=_cparams(("parallel",), (4 * kc * n_heads * 256 + 8 * tm * n_heads * 128) // 2 ** 20 + 16),
        name="kv_heads",
    )(ckv_b, w_ukv3)


def _gate_specs(x, mod, k_gate, tm, tn, grid_rank):
    b, s, d = x.shape
    x2 = x.reshape(b * s, d)
    ij = (lambda f: (lambda i, j: f(i, j))) if grid_rank == 2 else (lambda f: (lambda i, k, j: f(i, j)))
    x_spec = pl.BlockSpec((tm, tn), ij(lambda i, j: (i, j)))
    if s % tm == 0:
        per = s // tm
        g = (mod, pl.BlockSpec((None, None, 1, tn), ij(lambda i, j: (i // per, k_gate, 0, j))))
    else:
        rows = jnp.repeat(mod[:, k_gate, 0, :], s, axis=0)
        g = (rows, pl.BlockSpec((tm, tn), ij(lambda i, j: (i, j))))
    return [(x2, x_spec), g]


def _proj_residual(name, a2, w, layer, x, mod, k_gate):
    m, kk = a2.shape
    n = w.shape[2]
    b, s, _ = x.shape
    tm = _pick(s, 1024, 16) if s >= 256 else _pick(m, 1024, 16)
    tn = _pick(n, 512, 128)

    def epilogue(acc, ex, o):
        o[0][...] = ex[0][...] + ex[1][...] * acc

    out = _mm(name, (m // tm, n // tn), a2, pl.BlockSpec((tm, kk), lambda i, j: (i, 0)),
              w, pl.BlockSpec((None, kk, tn), lambda i, j: (layer, 0, j)), _gate_specs(x, mod, k_gate, tm, tn, 2),
              [(jax.ShapeDtypeStruct((m, n), F32), pl.BlockSpec((tm, tn), lambda i, j: (i, j)))],
              epilogue, sem=("parallel", "arbitrary"),
              vmem_mb=(4 * tm * kk + 10 * kk * tn + 20 * tm * tn) // 2 ** 20 + 8)[0]
    return out.reshape(b, s, n)


def _flash_kernel(qn_ref, qp_ref, kn_ref, v_ref, kpe_ref, o_ref, *, t, nq):
    nt = (((1,), (1,)), ((), ()))
    kpe = kpe_ref[0]
    qc = lax.broadcasted_iota(jnp.int32, (t, t), 0) // CHUNK
    kc = lax.broadcasted_iota(jnp.int32, (t, t), 1) // CHUNK
    diag_visible = kc <= qc
    for hh in range(2):
        hs = slice(hh * 128, (hh + 1) * 128)
        kcat = jnp.concatenate([kn_ref[0, :, hs], kpe], axis=-1)
        for qi in range(nq):
            lo = qi * t
            rows = slice(lo, lo + t)
            q = jnp.concatenate([qn_ref[0, rows, hs], qp_ref[0, rows, hh * 64:(hh + 1) * 64]], axis=-1)
            s_d = jnp.where(diag_visible, lax.dot_general(q, kcat[lo:lo + t], nt, preferred_element_type=F32), NEG_INF)
            m = jnp.max(s_d, axis=-1, keepdims=True)
            if qi > 0:
                s_o = lax.dot_general(q, kcat[:lo], nt, preferred_element_type=F32)
                m = jnp.maximum(m, jnp.max(s_o, axis=-1, keepdims=True))
                p_o = jnp.exp(s_o - m)
            p_d = jnp.exp(s_d - m)
            l = jnp.sum(p_d, axis=-1, keepdims=True)
            acc = jnp.dot(p_d.astype(BF16), v_ref[0, rows, hs], preferred_element_type=F32)
            if qi > 0:
                l = l + jnp.sum(p_o, axis=-1, keepdims=True)
                acc = acc + jnp.dot(p_o.astype(BF16), v_ref[0, 0:lo, hs], preferred_element_type=F32)
            o_ref[0, rows, hs] = (acc / l).astype(BF16)


def _flash(qn, qp, kn, v, kpe, n_heads):
    b, s, _ = qn.shape
    t = _pick(s, 256, CHUNK)
    kern = functools.partial(_flash_kernel, t=t, nq=s // t)
    wide = pl.BlockSpec((1, s, 256), lambda bb, hp: (bb, 0, hp))
    return pl.pallas_call(
        kern,
        grid=(b, n_heads // 2),
        in_specs=[wide, pl.BlockSpec((1, s, 128), lambda bb, hp: (bb, 0, hp)), wide, wide,
                  pl.BlockSpec((1, s, 64), lambda bb, hp: (bb, 0, 0))],
        out_specs=wide,
        out_shape=jax.ShapeDtypeStruct((b, s, n_heads * 128), BF16),
        compiler_params=_cparams(("parallel", "parallel"), 40),
        name="flash_prompt",
    )(qn, qp, kn, v, kpe)


def _sattn_kernel(ql_ref, qp_ref, ck_ref, kp_ref, nck_ref, nkp_ref, o_ref, m_sc, l_sc, acc_sc, *, past, n_heads):
    kb = pl.program_id(1)
    nt = (((1,), (1,)), ((), ()))

    @pl.when(kb == 0)
    def _():
        m_sc[...] = jnp.full(m_sc.shape, NEG_INF, F32)
        l_sc[...] = jnp.zeros(l_sc.shape, F32)
        acc_sc[...] = jnp.zeros(acc_sc.shape, F32)

    ql = ql_ref[0]
    qp = qp_ref[0]

    def update(ck, kp, mask):
        s = (lax.dot_general(ql, ck, nt, preferred_element_type=F32)
             + lax.dot_general(qp, kp, nt, preferred_element_type=F32))
        if mask is not None:
            s = jnp.where(mask, s, NEG_INF)
        m_prev = m_sc[...]
        m_new = jnp.maximum(m_prev, jnp.max(s, axis=-1, keepdims=True))
        alpha = jnp.exp(m_prev - m_new)
        p = jnp.exp(s - m_new)
        l_sc[...] = alpha * l_sc[...] + jnp.sum(p, axis=-1, keepdims=True)
        acc_sc[...] = alpha * acc_sc[...] + jnp.dot(p.astype(BF16), ck, preferred_element_type=F32)
        m_sc[...] = m_new

    update(ck_ref[0].astype(BF16), kp_ref[0].astype(BF16), None)

    @pl.when(kb == pl.num_programs(1) - 1)
    def _():
        rows, ds = ql.shape[0], nck_ref.shape[1]
        qpos = past + lax.broadcasted_iota(jnp.int32, (rows, ds), 0) // n_heads
        kpos = past + lax.broadcasted_iota(jnp.int32, (rows, ds), 1)
        update(nck_ref[0].astype(BF16), nkp_ref[0].astype(BF16), kpos // CHUNK <= qpos // CHUNK)
        o_ref[0] = (acc_sc[...] / l_sc[...]).astype(BF16)


def _sample_attention(q_lat, q_pe, cache_ckv, cache_kpe, layer, ckv_new, kpe_new, n_heads):
    b, rows, c = q_lat.shape
    past = cache_ckv.shape[2]
    ds = ckv_new.shape[1]
    tk = _pick(past, 1024, 128)
    kern = functools.partial(_sattn_kernel, past=past, n_heads=n_heads)
    return pl.pallas_call(
        kern,
        grid=(b, past // tk),
        in_specs=[pl.BlockSpec((1, rows, c), lambda bb, kb: (bb, 0, 0)),
                  pl.BlockSpec((1, rows, 64), lambda bb, kb: (bb, 0, 0)),
                  pl.BlockSpec((None, 1, tk, c), lambda bb, kb: (layer, bb, kb, 0)),
                  pl.BlockSpec((None, 1, tk, 64), lambda bb, kb: (layer, bb, kb, 0)),
                  pl.BlockSpec((1, ds, c), lambda bb, kb: (bb, 0, 0)),
                  pl.BlockSpec((1, ds, 64), lambda bb, kb: (bb, 0, 0))],
        out_specs=pl.BlockSpec((1, rows, c), lambda bb, kb: (bb, 0, 0)),
        out_shape=jax.ShapeDtypeStruct((b, rows, c), BF16),
        scratch_shapes=[pltpu.VMEM((rows, 1), F32), pltpu.VMEM((rows, 1), F32), pltpu.VMEM((rows, c), F32)],
        compiler_params=_cparams(("parallel", "arbitrary"), 40),
        name="attn_sample",
    )(q_lat, q_pe, cache_ckv, cache_kpe, ckv_new, kpe_new)


def _absorb_q(qn, w_ukv3, layer, n_heads):
    m = qn.shape[0]
    c = w_ukv3.shape[1]

    def epilogue(acc, ex, o):
        o[0][...] = acc.astype(BF16)

    return _mm("absorb_q", (n_heads,), qn, pl.BlockSpec((m, 128), lambda h: (0, h)),
               w_ukv3, pl.BlockSpec((None, c, 128), lambda h: (layer, 0, 2 * h)), [],
               [(jax.ShapeDtypeStruct((m, n_heads * c), BF16), pl.BlockSpec((m, c), lambda h: (0, h)))],
               epilogue, trans_b=True, sem=("parallel",), vmem_mb=16)[0]


def _expand_v(o_lat, w_ukv3, layer, n_heads):
    m = o_lat.shape[0]
    c = w_ukv3.shape[1]

    def epilogue(acc, ex, o):
        o[0][...] = acc.astype(BF16)

    return _mm("expand_v", (n_heads,), o_lat, pl.BlockSpec((m, c), lambda h: (0, h)),
               w_ukv3, pl.BlockSpec((None, c, 128), lambda h: (layer, 0, 2 * h + 1)), [],
               [(jax.ShapeDtypeStruct((m, n_heads * 128), BF16), pl.BlockSpec((m, 128), lambda h: (0, h)))],
               epilogue, sem=("parallel",), vmem_mb=16)[0]


def _pool_kernel(u_ref, prev_ref, w_ref, ps_ref, x_ref, g_ref, o_ref, st, *, ts, pos0, prev_rows, seq_tiles):
    sidx = pl.program_id(1)
    grp = pl.program_id(2)
    nb, gw = u_ref.shape[0], u_ref.shape[2]
    total = POOL_HALO + ts
    u = u_ref[...]
    if prev_rows == POOL_HALO:
        halo = prev_ref[...]
        st[0, :, 0:POOL_HALO, :] = jnp.where(sidx % seq_tiles == 0, jnp.zeros_like(halo), halo)
    else:
        st[0, :, 0:POOL_HALO, :] = jnp.zeros((nb, POOL_HALO, gw), F32)
        st[0, :, POOL_HALO - prev_rows:POOL_HALO, :] = prev_ref[...]
    st[0, :, POOL_HALO:total, :] = u
    for k in range(1, 5):
        @pl.when(grp >= k - 1)
        def _(k=k):
            half = 2 ** (k - 1)
            lo = 8 * k
            st[k, :, lo:total, :] = st[k - 1, :, lo:total, :] + st[k - 1, :, lo - half:total - half, :]
    win = st[grp + 1, :, POOL_HALO:total, :]
    wsize = jnp.left_shift(2, grp)
    pos = pos0 + (sidx % seq_tiles) * ts + lax.broadcasted_iota(jnp.int32, (nb, ts, gw), 1)
    cnt = jnp.minimum(pos + 1, wsize).astype(F32)
    d = (win / cnt - u).astype(BF16).reshape(nb * ts, gw)
    y = jnp.dot(d, w_ref[...].astype(BF16), preferred_element_type=F32).reshape(nb, ts, gw)
    o_ref[...] = x_ref[...] + g_ref[...] * (y * ps_ref[...])


def _pool_mix(u, prev, w_pool, pool_scale, layer_j, x, mod, k_gate, pos0):
    b, s, d = u.shape
    ng = w_pool.shape[1]
    gw = d // ng
    nb, ts = _row_blocks(b, s)
    if prev is None:
        nb, ts = 1, _pick(s, 512, POOL_HALO)
        seq_tiles = s // ts
        per = ts // POOL_HALO
        prev_arr, prev_rows = u, POOL_HALO
        prev_spec = pl.BlockSpec((1, POOL_HALO, gw), lambda i, j, g: (i, jnp.maximum(j * per - 1, 0), g))
    else:
        nb = _pick(b, 8, 1)
        seq_tiles = 1
        prev_rows = prev.shape[2]
        prev_arr = prev
        prev_spec = pl.BlockSpec((None, nb, prev_rows, gw), lambda i, j, g: (layer_j, i, 0, g))
    kern = functools.partial(_pool_kernel, ts=ts, pos0=pos0, prev_rows=prev_rows, seq_tiles=seq_tiles)
    blk = pl.BlockSpec((nb, ts, gw), lambda i, j, g: (i, j, g))
    return pl.pallas_call(
        kern,
        grid=(b // nb, s // ts, ng),
        in_specs=[blk, prev_spec,
                  pl.BlockSpec((None, None, gw, gw), lambda i, j, g: (layer_j, g, 0, 0)),
                  pl.BlockSpec((None, 1, 1, gw), lambda i, j, g: (layer_j, 0, 0, g)),
                  blk,
                  pl.BlockSpec((nb, None, 1, gw), lambda i, j, g: (i, k_gate, 0, g))],
        out_specs=blk,
        out_shape=jax.ShapeDtypeStruct((b, s, d), F32),
        scratch_shapes=[pltpu.VMEM((5, nb, POOL_HALO + ts, gw), F32)],
        compiler_params=_cparams(("parallel", "arbitrary", "arbitrary"), 48),
        name="pool_mix",
    )(u, prev_arr, w_pool, pool_scale.reshape(pool_scale.shape[0], 1, 1, d), x, mod)


def _conv_gate(hs_g, hs_v, base, rows, cw_g, cw_v, cb_g, cb_v, seq_axis):
    def conv(hs, cw, cb):
        def sl(off):
            idx = [slice(None)] * len(hs.shape)
            idx[seq_axis] = slice(base - off, base - off + rows)
            return hs[tuple(idx)]
        return cb[...] + sl(2) * cw[0:1, :] + sl(1) * cw[1:2, :] + sl(0) * cw[2:3, :]
    return _silu(conv(hs_g, cw_g, cb_g)) * conv(hs_v, cw_v, cb_v)


def _zero_pad_columns(body, act_ref, col_axis, nf):
    j = pl.program_id(col_axis)
    pl.when(j < nf)(body)

    @pl.when(j >= nf)
    def _():
        act_ref[...] = jnp.zeros(act_ref.shape, act_ref.dtype)


def _up_prompt_kernel(u_ref, wg_ref, wv_ref, cwg_ref, cwv_ref, cbg_ref, cbv_ref, act_ref, sg_ref, sv_ref, *, nf, **kw):
    body = functools.partial(_up_prompt_body, u_ref, wg_ref, wv_ref, cwg_ref, cwv_ref, cbg_ref, cbv_ref,
                             act_ref, sg_ref, sv_ref, **kw)
    _zero_pad_columns(body, act_ref, 1, nf)


def _up_prompt_body(u_ref, wg_ref, wv_ref, cwg_ref, cwv_ref, cbg_ref, cbv_ref, act_ref, sg_ref, sv_ref, *, tm, n_sub):
    tf = wg_ref.shape[1]
    streams = ((wg_ref[...].astype(BF16), cwg_ref, cbg_ref), (wv_ref[...].astype(BF16), cwv_ref, cbv_ref))
    sub = lax.broadcasted_iota(jnp.int32, (8, tf), 0)
    tails = [[jnp.zeros((8, tf), F32)] * 2 for _ in streams]
    for r in range(n_sub):
        a = u_ref[0, r * tm:(r + 1) * tm, :]
        conv = []
        for si, (w, cw, cb) in enumerate(streams):
            h = jnp.dot(a, w, preferred_element_type=F32)
            acc = cb[...] + h * cw[2:3, :]
            for k in (1, 2):
                rk = pltpu.roll(h, k, 0)
                hk = jnp.concatenate([jnp.where(sub < k, tails[si][k - 1], rk[:8]), rk[8:]], axis=0)
                tails[si][k - 1] = rk[:8]
                acc = acc + hk * cw[2 - k:3 - k, :]
            conv.append(acc)
        act_ref[0, r * tm:(r + 1) * tm, :] = (_silu(conv[0]) * conv[1]).astype(BF16)
    sg_ref[0] = tails[0][1][0:2]
    sv_ref[0] = tails[1][1][0:2]


def _up_sample_kernel(u_ref, pg_ref, pv_ref, wg_ref, wv_ref, cwg_ref, cwv_ref, cbg_ref, cbv_ref,
                      act_ref, sg_ref, sv_ref, hs_g, hs_v, *, nf, **kw):
    body = functools.partial(_up_sample_body, u_ref, pg_ref, pv_ref, wg_ref, wv_ref, cwg_ref, cwv_ref, cbg_ref, cbv_ref,
                             act_ref, sg_ref, sv_ref, hs_g, hs_v, **kw)
    _zero_pad_columns(body, act_ref, 0, nf)


def _up_sample_body(u_ref, pg_ref, pv_ref, wg_ref, wv_ref, cwg_ref, cwv_ref, cbg_ref, cbv_ref,
                    act_ref, sg_ref, sv_ref, hs_g, hs_v, *, nb, ds):
    a = u_ref[...]
    for w_ref, p_ref, hs, s_ref in ((wg_ref, pg_ref, hs_g, sg_ref), (wv_ref, pv_ref, hs_v, sv_ref)):
        h = jnp.dot(a, w_ref[...].astype(BF16), preferred_element_type=F32)
        hs[:, 6:8, :] = p_ref[...]
        hs[:, 8:8 + ds, :] = h.reshape(nb, ds, h.shape[-1])
        s_ref[...] = hs[:, 6 + ds:8 + ds, :]
    act = _conv_gate(hs_g, hs_v, 8, ds, cwg_ref, cwv_ref, cbg_ref, cbv_ref, 1)
    act_ref[...] = act.reshape(nb * ds, act.shape[-1]).astype(BF16)


def _ffn_up(h, w_up, conv_w, conv_b, layer, state):
    b, s, d = h.shape
    f2 = w_up.shape[2]
    f = f2 // 2
    tf = _pick(f, 256, 128)
    nf = f // tf
    tk = _down_chunk(f)
    f_pad = pl.cdiv(f, tk) * tk
    nf_pad = f_pad // tf
    cb3 = conv_b.reshape(conv_b.shape[0], 1, f2)
    w_specs = lambda im_g, im_v: [pl.BlockSpec((None, d, tf), im_g), pl.BlockSpec((None, d, tf), im_v)]
    if state is None:
        n_sub = 4 if s % 64 == 0 else 1
        tm = s // n_sub
        g3 = lambda bb, j: (layer, 0, jnp.minimum(j, nf - 1))
        v3 = lambda bb, j: (layer, 0, nf + jnp.minimum(j, nf - 1))
        kern = functools.partial(_up_prompt_kernel, nf=nf, tm=tm, n_sub=n_sub)
        st_spec = pl.BlockSpec((1, 2, tf), lambda bb, j: (bb, 0, jnp.minimum(j, nf - 1)))
        act, sg, sv = pl.pallas_call(
            kern,
            grid=(b, nf_pad),
            in_specs=[pl.BlockSpec((1, s, d), lambda bb, j: (bb, 0, 0), pipeline_mode=pl.Buffered(1))]
                     + w_specs(g3, v3)
                     + [pl.BlockSpec((None, 3, tf), g3), pl.BlockSpec((None, 3, tf), v3),
                        pl.BlockSpec((None, 1, tf), g3), pl.BlockSpec((None, 1, tf), v3)],
            out_specs=[pl.BlockSpec((1, s, tf), lambda bb, j: (bb, 0, j)), st_spec, st_spec],
            out_shape=[jax.ShapeDtypeStruct((b, s, f_pad), BF16),
                       jax.ShapeDtypeStruct((b, 2, f), F32), jax.ShapeDtypeStruct((b, 2, f), F32)],
            compiler_params=_cparams(("parallel", "arbitrary"), (2 * s * d + 24 * d * tf + 32 * s * tf) // 2 ** 20 + 8),
            name="ffn_up_prompt",
        )(h, w_up, w_up, conv_w, conv_w, cb3, cb3)
    else:
        g2 = lambda j: (layer, 0, jnp.minimum(j, nf - 1))
        v2 = lambda j: (layer, 0, nf + jnp.minimum(j, nf - 1))
        kern = functools.partial(_up_sample_kernel, nf=nf, nb=b, ds=s)
        st_spec = pl.BlockSpec((b, 2, tf), lambda j: (0, 0, jnp.minimum(j, nf - 1)))
        act, sg, sv = pl.pallas_call(
            kern,
            grid=(nf_pad,),
            in_specs=[pl.BlockSpec((b * s, d), lambda j: (0, 0)),
                      pl.BlockSpec((None, b, 2, tf), lambda j: (layer, 0, 0, jnp.minimum(j, nf - 1))),
                      pl.BlockSpec((None, b, 2, tf), lambda j: (layer, 0, 0, nf + jnp.minimum(j, nf - 1)))]
                     + w_specs(g2, v2)
                     + [pl.BlockSpec((None, 3, tf), g2), pl.BlockSpec((None, 3, tf), v2),
                        pl.BlockSpec((None, 1, tf), g2), pl.BlockSpec((None, 1, tf), v2)],
            out_specs=[pl.BlockSpec((b * s, tf), lambda j: (0, j)), st_spec, st_spec],
            out_shape=[jax.ShapeDtypeStruct((b * s, f_pad), BF16),
                       jax.ShapeDtypeStruct((b, 2, f), F32), jax.ShapeDtypeStruct((b, 2, f), F32)],
            scratch_shapes=[pltpu.VMEM((b, 8 + s, tf), F32), pltpu.VMEM((b, 8 + s, tf), F32)],
            compiler_params=_cparams(("arbitrary",), 32),
            name="ffn_up_sample",
        )(h.reshape(b * s, d), state, state, w_up, w_up, conv_w, conv_w, cb3, cb3)
    return act.reshape(b * s, f_pad), jnp.concatenate([sg, sv], axis=-1)


DOWN_K_CHUNK = 2816


def _down_chunk(f):
    return DOWN_K_CHUNK if f > DOWN_K_CHUNK else f


def _down_kernel(a_ref, w_ref, x_ref, g_ref, o_ref, acc, *, kdim, tk):
    k = pl.program_id(1)
    j = pl.program_id(2)
    w = w_ref[...]
    if kdim % tk:
        w = jnp.where(lax.broadcasted_iota(jnp.int32, w.shape, 0) < kdim - k * tk, w, jnp.zeros_like(w))

    part = jnp.dot(a_ref[...], w.astype(BF16), preferred_element_type=F32)

    @pl.when(k == 0)
    def _():
        acc[j] = part

    @pl.when(k > 0)
    def _():
        acc[j] += part

    @pl.when(k == pl.num_programs(1) - 1)
    def _():
        o_ref[...] = x_ref[...] + g_ref[...] * acc[j]


def _ffn_down(act, w_down, layer, x, mod, k_gate):
    m, f_pad = act.shape
    f = w_down.shape[1]
    b, s, d = x.shape
    tm = _pick(s, 1024, 16) if s >= 256 else _pick(m, 1024, 16)
    tn = _pick(d, 512, 128)
    tk = _down_chunk(f)
    nk = f_pad // tk
    nj = d // tn
    extras = _gate_specs(x, mod, k_gate, tm, tn, 3)
    out = pl.pallas_call(
        functools.partial(_down_kernel, kdim=f, tk=tk),
        grid=(m // tm, nk, nj),
        in_specs=[pl.BlockSpec((tm, tk), lambda i, k, j: (i, k)),
                  pl.BlockSpec((None, tk, tn), lambda i, k, j: (layer, k, j))] + [sp for _, sp in extras],
        out_specs=pl.BlockSpec((tm, tn), lambda i, k, j: (i, jnp.where(k == nk - 1, j, 0))),
        out_shape=jax.ShapeDtypeStruct((m, d), F32),
        scratch_shapes=[pltpu.VMEM((nj, tm, tn), F32)],
        compiler_params=_cparams(("parallel", "arbitrary", "arbitrary"),
                                 (4 * tm * d + 2 * 2 * tm * tk + 10 * tk * tn + 28 * tm * tn) // 2 ** 20 + 4),
        name="ffn_down",
    )(act, w_down, *[arr for arr, _ in extras])
    return out.reshape(b, s, d)


def kernel(x_prompt, x_sample, c_prompt, c_sample, cache_ckv, cache_kpe, state_pool, state_conv, w_ada, b_ada,
           norm_mix_w, norm_ffn_w, w_dq, q_norm_w, w_uq, w_dkv, kv_norm_w, w_ukv, w_o, w_pool, pool_scale,
           w_up, conv_w, conv_b, w_down, final_norm_w):
    bp, s, d = x_prompt.shape
    bs, ds, _ = x_sample.shape
    past = cache_ckv.shape[2]
    depth = w_ada.shape[0]
    n_heads = w_uq.shape[2]
    q_lora = w_dq.shape[2]
    kv_lora = w_dkv.shape[2] - 64
    qk_head = w_uq.shape[3]
    scale = float(qk_head) ** -0.5
    pos_p = jnp.arange(s, dtype=jnp.int32)
    pos_s = jnp.tile(past + jnp.arange(ds, dtype=jnp.int32), bs)

    nb_all = bp + bs
    pad = (-nb_all) % 16
    c_all = jnp.concatenate([c_prompt, c_sample, jnp.zeros((pad, d), F32)], axis=0)
    mod_all = _ada(c_all, w_ada, b_ada)

    xp, xs = x_prompt, x_sample
    ckv_p_l, kpe_p_l, ckv_s_l, kpe_s_l = [], [], [], []
    pool_p_l, pool_s_l, conv_p_l, conv_s_l = [], [], [], []
    for i in range(depth):
        j = i // 2
        mod_p = mod_all[i, :bp].reshape(bp, 6, 1, d)
        mod_s = mod_all[i, bp:nb_all].reshape(bs, 6, 1, d)
        if i % 2 == 0:
            w_in = jnp.concatenate([w_dq[j], w_dkv[j], jnp.zeros((d, 64), F32)], axis=1).astype(BF16)
            w_q_nope = w_uq[j][:, :, :128].reshape(q_lora, n_heads * 128).astype(BF16)
            w_q_pe = w_uq[j][:, :, 128:].reshape(q_lora, n_heads * 64).astype(BF16)
            w_ukv3 = w_ukv.reshape(w_ukv.shape[0], kv_lora, n_heads * 256)
            up = _normmod(xp, norm_mix_w, i, mod_p, 0, 1, BF16)
            cq, ckv_p, ckv_pb, kpe_p, kpe_pb = _mla_project(up.reshape(bp * s, d), pos_p, w_in, q_norm_w, kv_norm_w, j,
                                                            q_lora, kv_lora)
            qn, qp = _q_heads(cq, w_q_nope, w_q_pe, pos_p, scale)
            kn, vv = _kv_heads(ckv_pb, w_ukv3, j, n_heads)
            r3 = lambda t: t.reshape(bp, s, t.shape[-1])
            o_p = _flash(r3(qn), r3(qp), r3(kn), r3(vv), r3(kpe_pb), n_heads)
            xp = _proj_residual("attn_out_prompt", o_p.reshape(bp * s, n_heads * 128), w_o, j, xp, mod_p, 2)
            us = _normmod(xs, norm_mix_w, i, mod_s, 0, 1, BF16)
            cq, ckv_s, _, kpe_s, _ = _mla_project(us.reshape(bs * ds, d), pos_s, w_in, q_norm_w, kv_norm_w, j,
                                                  q_lora, kv_lora)
            qn, qp = _q_heads(cq, w_q_nope, w_q_pe, pos_s, scale)
            q_lat = _absorb_q(qn, w_ukv3, j, n_heads)
            o_lat = _sample_attention(q_lat.reshape(bs, ds * n_heads, kv_lora), qp.reshape(bs, ds * n_heads, 64),
                                      cache_ckv, cache_kpe, j, ckv_s.reshape(bs, ds, kv_lora),
                                      kpe_s.reshape(bs, ds, 64), n_heads)
            o_s = _expand_v(o_lat.reshape(bs * ds, n_heads * kv_lora), w_ukv3, j, n_heads)
            xs = _proj_residual("attn_out_sample", o_s, w_o, j, xs, mod_s, 2)
            ckv_p_l.append(ckv_p.reshape(bp, s, kv_lora))
            kpe_p_l.append(kpe_p.reshape(bp, s, 64))
            ckv_s_l.append(ckv_s.reshape(bs, ds, kv_lora))
            kpe_s_l.append(kpe_s.reshape(bs, ds, 64))
        else:
            hist = state_pool.shape[2]
            up = _normmod(xp, norm_mix_w, i, mod_p, 0, 1, F32)
            us = _normmod(xs, norm_mix_w, i, mod_s, 0, 1, F32)
            pool_p_l.append(up[:, s - hist:])
            pool_s_l.append(jnp.concatenate([state_pool[j], us], axis=1)[:, -hist:])
            xp = _pool_mix(up, None, w_pool, pool_scale, j, xp, mod_p, 2, 0)
            xs = _pool_mix(us, state_pool, w_pool, pool_scale, j, xs, mod_s, 2, past)
        hp = _normmod(xp, norm_ffn_w, i, mod_p, 3, 4, BF16)
        hs = _normmod(xs, norm_ffn_w, i, mod_s, 3, 4, BF16)
        act_p, cst_p = _ffn_up(hp, w_up, conv_w, conv_b, i, None)
        act_s, cst_s = _ffn_up(hs, w_up, conv_w, conv_b, i, state_conv)
        conv_p_l.append(cst_p)
        conv_s_l.append(cst_s)
        xp = _ffn_down(act_p, w_down, i, xp, mod_p, 5)
        xs = _ffn_down(act_s, w_down, i, xs, mod_s, 5)

    y_prompt = _final_norm(xp, final_norm_w)
    y_sample = _final_norm(xs, final_norm_w)
    return (y_prompt, y_sample,
            jnp.stack(ckv_p_l), jnp.stack(kpe_p_l), jnp.stack(ckv_s_l), jnp.stack(kpe_s_l),
            jnp.stack(pool_p_l), jnp.stack(pool_s_l), jnp.stack(conv_p_l), jnp.stack(conv_s_l))
```

```python
import functools

import jax
import jax.numpy as jnp
from jax import lax
from jax.experimental import pallas as pl
from jax.experimental.pallas import tpu as pltpu

F32 = jnp.float32
BF16 = jnp.bfloat16

CHUNK = 64
ROPE_THETA = 10000.0
RMS_EPS = 1e-6
NEG_INF = -1e30
POOL_WINDOWS = (2, 4, 8, 16)
POOL_HALO = 32
V7X_VMEM_CAP_MB = 56


def _cparams(sem, vmem_mb, flags=None):
    return pltpu.CompilerParams(dimension_semantics=sem, vmem_limit_bytes=min(vmem_mb, V7X_VMEM_CAP_MB) * 2 ** 20,
                                flags=flags)


def _pick(dim, pref, align):
    t = min(pref, dim)
    t -= t % align
    while t >= align:
        if dim % t == 0:
            return t
        t -= align
    return dim


def _silu(x):
    return x * (1.0 / (1.0 + jnp.exp(-x)))


def _rms(x, w):
    return x * lax.rsqrt(jnp.mean(x * x, axis=-1, keepdims=True) + RMS_EPS) * w


def _rope_rot(x):
    n = x.shape[-1]
    lane = lax.broadcasted_iota(jnp.int32, x.shape, x.ndim - 1)
    return jnp.where(lane % 64 < 32, -pltpu.roll(x, n - 32, x.ndim - 1), pltpu.roll(x, 32, x.ndim - 1))


def _rope(x, cos_ref, sin_ref):
    reps = x.shape[-1] // 128
    c = cos_ref[...]
    s = sin_ref[...]
    if reps > 1:
        c = jnp.concatenate([c] * reps, axis=-1)
        s = jnp.concatenate([s] * reps, axis=-1)
    return x * c + _rope_rot(x) * s


def _ada_kernel(c_ref, w_ref, b_ref, o_ref):
    a = _silu(c_ref[...]).astype(BF16)
    o_ref[...] = jnp.dot(a, w_ref[...].astype(BF16), preferred_element_type=F32) + b_ref[...]


def _ada(c, w_ada, b_ada):
    n_layers, d, n = w_ada.shape
    bc = c.shape[0]
    tn = _pick(n, 512, 128)
    return pl.pallas_call(
        _ada_kernel,
        grid=(n_layers, n // tn),
        in_specs=[pl.BlockSpec((bc, d), lambda l, j: (0, 0)),
                  pl.BlockSpec((None, d, tn), lambda l, j: (l, 0, j)),
                  pl.BlockSpec((None, 1, tn), lambda l, j: (l, 0, j))],
        out_specs=pl.BlockSpec((None, bc, tn), lambda l, j: (l, 0, j)),
        out_shape=jax.ShapeDtypeStruct((n_layers, bc, n), F32),
        compiler_params=_cparams(("parallel", "parallel"), 4 * d * tn * 3 // 2 ** 20 + 8),
        name="ada",
    )(c, w_ada, b_ada.reshape(n_layers, 1, n))


def _normmod_kernel(x_ref, w_ref, sh_ref, sc_ref, o_ref):
    y = _rms(x_ref[...], w_ref[...])
    o_ref[...] = (y * (1.0 + sc_ref[...]) + sh_ref[...]).astype(o_ref.dtype)


def _norm_kernel(x_ref, w_ref, o_ref):
    o_ref[...] = _rms(x_ref[...], w_ref[...]).astype(o_ref.dtype)


def _row_blocks(b, s):
    if s >= 256:
        return 1, _pick(s, 256, 16)
    return _pick(b, max(1, 256 // s), 1), s


def _normmod(x, w, layer, mod, k_shift, k_scale, out_dtype):
    b, s, d = x.shape
    nb, ts = _row_blocks(b, s)
    w3 = w.reshape(w.shape[0], 1, d)
    return pl.pallas_call(
        _normmod_kernel,
        grid=(b // nb, s // ts),
        in_specs=[pl.BlockSpec((nb, ts, d), lambda i, j: (i, j, 0)),
                  pl.BlockSpec((1, 1, d), lambda i, j: (layer, 0, 0)),
                  pl.BlockSpec((nb, None, 1, d), lambda i, j: (i, k_shift, 0, 0)),
                  pl.BlockSpec((nb, None, 1, d), lambda i, j: (i, k_scale, 0, 0))],
        out_specs=pl.BlockSpec((nb, ts, d), lambda i, j: (i, j, 0)),
        out_shape=jax.ShapeDtypeStruct((b, s, d), out_dtype),
        compiler_params=_cparams(("parallel", "parallel"), 32),
        name="normmod",
    )(x, w3, mod, mod)


def _final_norm(x, w):
    b, s, d = x.shape
    nb, ts = _row_blocks(b, s)
    return pl.pallas_call(
        _norm_kernel,
        grid=(b // nb, s // ts),
        in_specs=[pl.BlockSpec((nb, ts, d), lambda i, j: (i, j, 0)),
                  pl.BlockSpec((1, 1, d), lambda i, j: (0, 0, 0))],
        out_specs=pl.BlockSpec((nb, ts, d), lambda i, j: (i, j, 0)),
        out_shape=jax.ShapeDtypeStruct((b, s, d), F32),
        compiler_params=_cparams(("parallel", "parallel"), 32),
        name="final_norm",
    )(x, w.reshape(1, 1, d))


def _mm(name, grid, a, a_spec, b, b_spec, extras, outs, epilogue, *, trans_b=False, sem, vmem_mb):
    n_ex, n_out = len(extras), len(outs)
    dn = (((1,), (1 if trans_b else 0,)), ((), ()))

    def kern(*refs):
        a_blk = refs[0][...]
        a_blk = a_blk.reshape(-1, a_blk.shape[-1]).astype(BF16)
        acc = lax.dot_general(a_blk, refs[1][...].astype(BF16), dn, preferred_element_type=F32)
        epilogue(acc, refs[2:2 + n_ex], refs[2 + n_ex:2 + n_ex + n_out])

    return pl.pallas_call(
        kern,
        grid=grid,
        in_specs=[a_spec, b_spec] + [s for _, s in extras],
        out_specs=[s for _, s in outs],
        out_shape=[sd for sd, _ in outs],
        compiler_params=_cparams(sem, vmem_mb),
        name=name,
    )(a, b, *[x for x, _ in extras])


def _rope_tables(pos):
    half = 32
    inv = jnp.power(ROPE_THETA, -jnp.arange(half, dtype=F32) / half)
    ang = pos.astype(F32)[:, None] * inv[None, :]
    return jnp.tile(jnp.cos(ang), (1, 4)), jnp.tile(jnp.sin(ang), (1, 4))


def _mla_project(u2, pos_rows, w_in, q_norm_w, kv_norm_w, layer, q_lora, kv_lora):
    m, d = u2.shape
    n = w_in.shape[1]
    period = pos_rows.shape[0]
    tm = _pick(period, 512, 16)
    cos, sin = _rope_tables(pos_rows)
    npb = period // tm

    def epilogue(acc, ex, o):
        qw_ref, kw_ref, cos_ref, sin_ref = ex
        o[0][...] = _rms(acc[:, :q_lora], qw_ref[...]).astype(BF16)
        ckv = _rms(acc[:, q_lora:q_lora + kv_lora], kw_ref[...])
        o[1][...] = ckv
        o[2][...] = ckv.astype(BF16)
        kpe = _rope(acc[:, q_lora + kv_lora:], cos_ref, sin_ref)[:, :64]
        o[3][...] = kpe
        o[4][...] = kpe.astype(BF16)

    row = lambda width: pl.BlockSpec((tm, width), lambda i: (i, 0))
    return _mm(
        "mla_in", (m // tm,), u2, pl.BlockSpec((tm, d), lambda i: (i, 0)),
        w_in, pl.BlockSpec((d, n), lambda i: (0, 0)),
        [(q_norm_w.reshape(-1, 1, q_lora), pl.BlockSpec((None, 1, q_lora), lambda i: (layer, 0, 0))),
         (kv_norm_w.reshape(-1, 1, kv_lora), pl.BlockSpec((None, 1, kv_lora), lambda i: (layer, 0, 0))),
         (cos, pl.BlockSpec((tm, 128), lambda i: (i % npb, 0))),
         (sin, pl.BlockSpec((tm, 128), lambda i: (i % npb, 0)))],
        [(jax.ShapeDtypeStruct((m, q_lora), BF16), row(q_lora)),
         (jax.ShapeDtypeStruct((m, kv_lora), F32), row(kv_lora)),
         (jax.ShapeDtypeStruct((m, kv_lora), BF16), row(kv_lora)),
         (jax.ShapeDtypeStruct((m, 64), F32), row(64)),
         (jax.ShapeDtypeStruct((m, 64), BF16), row(64))],
        epilogue, sem=("parallel",), vmem_mb=(4 * d * n + 4 * tm * d + 16 * tm * n) // 2 ** 20 + 8)


def _q_heads(cq, w_nope, w_pe, pos_rows, scale):
    m, kq = cq.shape
    period = pos_rows.shape[0]
    tm = _pick(period, 512, 16)
    npb = period // tm
    cos, sin = _rope_tables(pos_rows)

    def plain(acc, ex, o):
        o[0][...] = (acc * scale).astype(BF16)

    def roped(acc, ex, o):
        o[0][...] = (_rope(acc, ex[0], ex[1]) * scale).astype(BF16)

    outs = []
    for name, w, extras, epi in (("q_nope", w_nope, [], plain),
                                 ("q_pe", w_pe, [(cos, pl.BlockSpec((tm, 128), lambda i, j: (i % npb, 0))),
                                                 (sin, pl.BlockSpec((tm, 128), lambda i, j: (i % npb, 0)))], roped)):
        n = w.shape[1]
        tn = _pick(n, 2048, 128)
        outs.append(_mm(name, (m // tm, n // tn), cq, pl.BlockSpec((tm, kq), lambda i, j: (i, 0)),
                        w, pl.BlockSpec((kq, tn), lambda i, j: (0, j)), extras,
                        [(jax.ShapeDtypeStruct((m, n), BF16), pl.BlockSpec((tm, tn), lambda i, j: (i, j)))],
                        epi, sem=("parallel", "parallel"), vmem_mb=40)[0])
    return outs


def _kv_heads(ckv_b, w_ukv3, layer, n_heads):
    m, kc = ckv_b.shape
    tm = _pick(m, 512, 16)
    hg = 4 if n_heads % 4 == 0 else 1

    def kern(c_ref, w_ref, k_ref, v_ref):
        c = c_ref[...]
        for g in range(n_heads // hg):
            acc = jnp.dot(c, w_ref[:, g * hg * 256:(g + 1) * hg * 256].astype(BF16), preferred_element_type=F32)
            for h in range(hg):
                cols = slice((g * hg + h) * 128, (g * hg + h + 1) * 128)
                k_ref[:, cols] = acc[:, h * 256:h * 256 + 128].astype(BF16)
                v_ref[:, cols] = acc[:, h * 256 + 128:(h + 1) * 256].astype(BF16)

    out = jax.ShapeDtypeStruct((m, n_heads * 128), BF16)
    out_spec = pl.BlockSpec((tm, n_heads * 128), lambda i: (i, 0))
    return pl.pallas_call(
        kern,
        grid=(m // tm,),
        in_specs=[pl.BlockSpec((tm, kc), lambda i: (i, 0)),
                  pl.BlockSpec((None, kc, n_heads * 256), lambda i: (layer, 0, 0), pipeline_mode=pl.Buffered(1))],
        out_specs=[out_spec, out_spec],
        out_shape=[out, out],
        compiler_params=_cparams(("parallel",), (4 * kc * n_heads * 256 + 8 * tm * n_heads * 128) // 2 ** 20 + 16),
        name="kv_heads",
    )(ckv_b, w_ukv3)


def _gate_specs(x, mod, k_gate, tm, tn, grid_rank):
    b, s, d = x.shape
    x2 = x.reshape(b * s, d)
    ij = (lambda f: (lambda i, j: f(i, j))) if grid_rank == 2 else (lambda f: (lambda i, j, k: f(i, j)))
    x_spec = pl.BlockSpec((tm, tn), ij(lambda i, j: (i, j)))
    if s % tm == 0:
        per = s // tm
        g = (mod, pl.BlockSpec((None, None, 1, tn), ij(lambda i, j: (i // per, k_gate, 0, j))))
    else:
        rows = jnp.repeat(mod[:, k_gate, 0, :], s, axis=0)
        g = (rows, pl.BlockSpec((tm, tn), ij(lambda i, j: (i, j))))
    return [(x2, x_spec), g]


def _proj_residual(name, a2, w, layer, x, mod, k_gate):
    m, kk = a2.shape
    n = w.shape[2]
    b, s, _ = x.shape
    tm = _pick(s, 1024, 16) if s >= 256 else _pick(m, 1024, 16)
    tn = _pick(n, 512, 128)

    def epilogue(acc, ex, o):
        o[0][...] = ex[0][...] + ex[1][...] * acc

    out = _mm(name, (m // tm, n // tn), a2, pl.BlockSpec((tm, kk), lambda i, j: (i, 0)),
              w, pl.BlockSpec((None, kk, tn), lambda i, j: (layer, 0, j)), _gate_specs(x, mod, k_gate, tm, tn, 2),
              [(jax.ShapeDtypeStruct((m, n), F32), pl.BlockSpec((tm, tn), lambda i, j: (i, j)))],
              epilogue, sem=("parallel", "arbitrary"),
              vmem_mb=(4 * tm * kk + 10 * kk * tn + 20 * tm * tn) // 2 ** 20 + 8)[0]
    return out.reshape(b, s, n)


def _flash_kernel(qn_ref, qp_ref, kn_ref, v_ref, kpe_ref, o_ref, *, t, nq):
    nt = (((1,), (1,)), ((), ()))
    kpe = kpe_ref[0]
    qc = lax.broadcasted_iota(jnp.int32, (t, t), 0) // CHUNK
    kc = lax.broadcasted_iota(jnp.int32, (t, t), 1) // CHUNK
    diag_visible = kc <= qc
    for hh in range(2):
        hs = slice(hh * 128, (hh + 1) * 128)
        kcat = jnp.concatenate([kn_ref[0, :, hs], kpe], axis=-1)
        for qi in range(nq):
            lo = qi * t
            rows = slice(lo, lo + t)
            q = jnp.concatenate([qn_ref[0, rows, hs], qp_ref[0, rows, hh * 64:(hh + 1) * 64]], axis=-1)
            s_d = jnp.where(diag_visible, lax.dot_general(q, kcat[lo:lo + t], nt, preferred_element_type=F32), NEG_INF)
            m = jnp.max(s_d, axis=-1, keepdims=True)
            if qi > 0:
                s_o = lax.dot_general(q, kcat[:lo], nt, preferred_element_type=F32)
                m = jnp.maximum(m, jnp.max(s_o, axis=-1, keepdims=True))
                p_o = jnp.exp(s_o - m)
            p_d = jnp.exp(s_d - m)
            l = jnp.sum(p_d, axis=-1, keepdims=True)
            acc = jnp.dot(p_d.astype(BF16), v_ref[0, rows, hs], preferred_element_type=F32)
            if qi > 0:
                l = l + jnp.sum(p_o, axis=-1, keepdims=True)
                acc = acc + jnp.dot(p_o.astype(BF16), v_ref[0, 0:lo, hs], preferred_element_type=F32)
            o_ref[0, rows, hs] = (acc / l).astype(BF16)


def _flash(qn, qp, kn, v, kpe, n_heads):
    b, s, _ = qn.shape
    t = _pick(s, 256, CHUNK)
    kern = functools.partial(_flash_kernel, t=t, nq=s // t)
    wide = pl.BlockSpec((1, s, 256), lambda bb, hp: (bb, 0, hp))
    return pl.pallas_call(
        kern,
        grid=(b, n_heads // 2),
        in_specs=[wide, pl.BlockSpec((1, s, 128), lambda bb, hp: (bb, 0, hp)), wide, wide,
                  pl.BlockSpec((1, s, 64), lambda bb, hp: (bb, 0, 0))],
        out_specs=wide,
        out_shape=jax.ShapeDtypeStruct((b, s, n_heads * 128), BF16),
        compiler_params=_cparams(("parallel", "parallel"), 40),
        name="flash_prompt",
    )(qn, qp, kn, v, kpe)


def _sattn_kernel(ql_ref, qp_ref, ck_ref, kp_ref, nck_ref, nkp_ref, o_ref, *, past, n_split):
    nt = (((1,), (1,)), ((), ()))
    n_heads, ds, c = ql_ref.shape[1:]
    ck = ck_ref[0].astype(BF16)
    kp = kp_ref[0].astype(BF16)
    nck = nck_ref[0].astype(BF16)
    nkp = nkp_ref[0].astype(BF16)
    hg = n_heads // n_split
    rows = hg * ds
    qpos = past + lax.broadcasted_iota(jnp.int32, (rows, ds), 0) % ds
    kpos = past + lax.broadcasted_iota(jnp.int32, (rows, ds), 1)
    new_visible = kpos // CHUNK <= qpos // CHUNK
    for g in range(n_split):
        ql = ql_ref[0, g * hg:(g + 1) * hg].reshape(rows, c)
        qp = qp_ref[0, g * hg:(g + 1) * hg].reshape(rows, qp_ref.shape[-1])
        s_c = lax.dot_general(ql, ck, nt, preferred_element_type=F32) + lax.dot_general(qp, kp, nt, preferred_element_type=F32)
        s_n = lax.dot_general(ql, nck, nt, preferred_element_type=F32) + lax.dot_general(qp, nkp, nt, preferred_element_type=F32)
        s_n = jnp.where(new_visible, s_n, NEG_INF)
        m = jnp.maximum(jnp.max(s_c, axis=-1, keepdims=True), jnp.max(s_n, axis=-1, keepdims=True))
        p_c = jnp.exp(s_c - m)
        p_n = jnp.exp(s_n - m)
        l = jnp.sum(p_c, axis=-1, keepdims=True) + jnp.sum(p_n, axis=-1, keepdims=True)
        acc = (jnp.dot(p_c.astype(BF16), ck, preferred_element_type=F32)
               + jnp.dot(p_n.astype(BF16), nck, preferred_element_type=F32))
        o_ref[0, g * hg:(g + 1) * hg] = (acc / l).astype(BF16).reshape(hg, ds, c)


def _sample_attention(q_lat, q_pe, cache_ckv, cache_kpe, layer, ckv_new, kpe_new):
    b, n_heads, ds, c = q_lat.shape
    past = cache_ckv.shape[2]
    n_split = 2 if n_heads % 2 == 0 else 1
    kern = functools.partial(_sattn_kernel, past=past, n_split=n_split)
    per_stream = lambda *blk: pl.BlockSpec((1,) + blk, lambda bb: (bb,) + (0,) * len(blk))
    rows = n_heads * ds // n_split
    return pl.pallas_call(
        kern,
        grid=(b,),
        in_specs=[per_stream(n_heads, ds, c), per_stream(n_heads, ds, 64),
                  pl.BlockSpec((None, 1, past, c), lambda bb: (layer, bb, 0, 0)),
                  pl.BlockSpec((None, 1, past, 64), lambda bb: (layer, bb, 0, 0)),
                  per_stream(ds, c), per_stream(ds, 64)],
        out_specs=per_stream(n_heads, ds, c),
        out_shape=jax.ShapeDtypeStruct((b, n_heads, ds, c), BF16),
        compiler_params=_cparams(("parallel",), (2 * 4 * past * (c + 128) + 2 * past * c + 12 * rows * past) // 2 ** 20 + 10),
        name="attn_sample",
    )(q_lat, q_pe, cache_ckv, cache_kpe, ckv_new, kpe_new)


def _absorb_q(qn, w_ukv3, layer, n_heads, b, ds):
    m = qn.shape[0]
    c = w_ukv3.shape[1]

    def epilogue(acc, ex, o):
        o[0][...] = acc.astype(BF16).reshape(b, ds, c)

    return _mm("absorb_q", (n_heads,), qn, pl.BlockSpec((m, 128), lambda h: (0, h)),
               w_ukv3, pl.BlockSpec((None, c, 128), lambda h: (layer, 0, 2 * h)), [],
               [(jax.ShapeDtypeStruct((b, n_heads, ds, c), BF16), pl.BlockSpec((b, None, ds, c), lambda h: (0, h, 0, 0)))],
               epilogue, trans_b=True, sem=("parallel",), vmem_mb=16)[0]


def _expand_v(o_lat, w_ukv3, layer):
    b, n_heads, ds, c = o_lat.shape
    m = b * ds

    def epilogue(acc, ex, o):
        o[0][...] = acc.astype(BF16)

    return _mm("expand_v", (n_heads,), o_lat, pl.BlockSpec((b, None, ds, c), lambda h: (0, h, 0, 0)),
               w_ukv3, pl.BlockSpec((None, c, 128), lambda h: (layer, 0, 2 * h + 1)), [],
               [(jax.ShapeDtypeStruct((m, n_heads * 128), BF16), pl.BlockSpec((m, 128), lambda h: (0, h)))],
               epilogue, sem=("parallel",), vmem_mb=16)[0]


def _pool_kernel(u_ref, prev_ref, w_ref, ps_ref, x_ref, g_ref, o_ref, st, *, ts, pos0, prev_rows, seq_tiles):
    sidx = pl.program_id(1)
    grp = pl.program_id(2)
    nb, gw = u_ref.shape[0], u_ref.shape[2]
    total = POOL_HALO + ts
    u = u_ref[...]
    if prev_rows == POOL_HALO:
        halo = prev_ref[...]
        st[0, :, 0:POOL_HALO, :] = jnp.where(sidx % seq_tiles == 0, jnp.zeros_like(halo), halo)
    else:
        st[0, :, 0:POOL_HALO, :] = jnp.zeros((nb, POOL_HALO, gw), F32)
        st[0, :, POOL_HALO - prev_rows:POOL_HALO, :] = prev_ref[...]
    st[0, :, POOL_HALO:total, :] = u
    for k in range(1, 5):
        @pl.when(grp >= k - 1)
        def _(k=k):
            half = 2 ** (k - 1)
            lo = 8 * k
            st[k, :, lo:total, :] = st[k - 1, :, lo:total, :] + st[k - 1, :, lo - half:total - half, :]
    win = st[grp + 1, :, POOL_HALO:total, :]
    wsize = jnp.left_shift(2, grp)
    pos = pos0 + (sidx % seq_tiles) * ts + lax.broadcasted_iota(jnp.int32, (nb, ts, gw), 1)
    cnt = jnp.minimum(pos + 1, wsize).astype(F32)
    d = (win / cnt - u).astype(BF16).reshape(nb * ts, gw)
    y = jnp.dot(d, w_ref[...].astype(BF16), preferred_element_type=F32).reshape(nb, ts, gw)
    o_ref[...] = x_ref[...] + g_ref[...] * (y * ps_ref[...])


def _pool_mix(u, prev, w_pool, pool_scale, layer_j, x, mod, k_gate, pos0):
    b, s, d = u.shape
    ng = w_pool.shape[1]
    gw = d // ng
    nb, ts = _row_blocks(b, s)
    if prev is None:
        nb, ts = 1, _pick(s, 512, POOL_HALO)
        seq_tiles = s // ts
        per = ts // POOL_HALO
        prev_arr, prev_rows = u, POOL_HALO
        prev_spec = pl.BlockSpec((1, POOL_HALO, gw), lambda i, j, g: (i, jnp.maximum(j * per - 1, 0), g))
    else:
        nb = _pick(b, 8, 1)
        seq_tiles = 1
        prev_rows = prev.shape[2]
        prev_arr = prev
        prev_spec = pl.BlockSpec((None, nb, prev_rows, gw), lambda i, j, g: (layer_j, i, 0, g))
    kern = functools.partial(_pool_kernel, ts=ts, pos0=pos0, prev_rows=prev_rows, seq_tiles=seq_tiles)
    blk = pl.BlockSpec((nb, ts, gw), lambda i, j, g: (i, j, g))
    return pl.pallas_call(
        kern,
        grid=(b // nb, s // ts, ng),
        in_specs=[blk, prev_spec,
                  pl.BlockSpec((None, None, gw, gw), lambda i, j, g: (layer_j, g, 0, 0)),
                  pl.BlockSpec((None, 1, 1, gw), lambda i, j, g: (layer_j, 0, 0, g)),
                  blk,
                  pl.BlockSpec((nb, None, 1, gw), lambda i, j, g: (i, k_gate, 0, g))],
        out_specs=blk,
        out_shape=jax.ShapeDtypeStruct((b, s, d), F32),
        scratch_shapes=[pltpu.VMEM((5, nb, POOL_HALO + ts, gw), F32)],
        compiler_params=_cparams(("parallel", "arbitrary", "arbitrary"), 48),
        name="pool_mix",
    )(u, prev_arr, w_pool, pool_scale.reshape(pool_scale.shape[0], 1, 1, d), x, mod)


def _conv_gate(hs_g, hs_v, base, rows, cw_g, cw_v, cb_g, cb_v, seq_axis):
    def conv(hs, cw, cb):
        def sl(off):
            idx = [slice(None)] * len(hs.shape)
            idx[seq_axis] = slice(base - off, base - off + rows)
            return hs[tuple(idx)]
        return cb[...] + sl(2) * cw[0:1, :] + sl(1) * cw[1:2, :] + sl(0) * cw[2:3, :]
    return _silu(conv(hs_g, cw_g, cb_g)) * conv(hs_v, cw_v, cb_v)


def _zero_pad_columns(body, act_ref, col_axis, nf):
    j = pl.program_id(col_axis)
    pl.when(j < nf)(body)

    @pl.when(j >= nf)
    def _():
        act_ref[...] = jnp.zeros(act_ref.shape, act_ref.dtype)


def _up_prompt_kernel(u_ref, wg_ref, wv_ref, cwg_ref, cwv_ref, cbg_ref, cbv_ref, act_ref, sg_ref, sv_ref, *, f, tm, n_sub):
    tf = wg_ref.shape[1]
    streams = ((wg_ref, cwg_ref, cbg_ref), (wv_ref, cwv_ref, cbv_ref))
    sub = lax.broadcasted_iota(jnp.int32, (8, tf), 0)
    in_range = lax.broadcasted_iota(jnp.int32, (tm, tf), 1) < f - pl.program_id(1) * tf
    tails = [[jnp.zeros((8, tf), F32)] * 2 for _ in streams]
    for r in range(n_sub):
        a = u_ref[0, r * tm:(r + 1) * tm, :]
        conv = []
        for si, (w_ref, cw, cb) in enumerate(streams):
            h = jnp.dot(a, w_ref[...], preferred_element_type=F32)
            acc = cb[...] + h * cw[2:3, :]
            for k in (1, 2):
                rk = pltpu.roll(h, k, 0)
                hk = jnp.concatenate([jnp.where(sub < k, tails[si][k - 1], rk[:8]), rk[8:]], axis=0)
                tails[si][k - 1] = rk[:8]
                acc = acc + hk * cw[2 - k:3 - k, :]
            conv.append(acc)
        act = _silu(conv[0]) * conv[1]
        if f % tf:
            act = jnp.where(in_range, act, 0.0)
        act_ref[0, r * tm:(r + 1) * tm, :] = act.astype(BF16)

    @pl.when(pl.program_id(1) * tf < f)
    def _():
        sg_ref[0] = tails[0][1][0:2]
        sv_ref[0] = tails[1][1][0:2]


def _up_sample_kernel(u_ref, pg_ref, pv_ref, wg_ref, wv_ref, cwg_ref, cwv_ref, cbg_ref, cbv_ref,
                      act_ref, sg_ref, sv_ref, wgb_ref, wvb_ref, hs_g, hs_v, *, nf, **kw):
    body = functools.partial(_up_sample_body, u_ref, pg_ref, pv_ref, wg_ref, wv_ref, cwg_ref, cwv_ref, cbg_ref, cbv_ref,
                             act_ref, sg_ref, sv_ref, wgb_ref, wvb_ref, hs_g, hs_v, **kw)
    _zero_pad_columns(body, act_ref, 0, nf)


def _up_sample_body(u_ref, pg_ref, pv_ref, wg_ref, wv_ref, cwg_ref, cwv_ref, cbg_ref, cbv_ref,
                    act_ref, sg_ref, sv_ref, wgb_ref, wvb_ref, hs_g, hs_v, *, nb, ds):
    a = u_ref[...]
    for w_ref, wb_ref, p_ref, hs, s_ref in ((wg_ref, wgb_ref, pg_ref, hs_g, sg_ref), (wv_ref, wvb_ref, pv_ref, hs_v, sv_ref)):
        w = w_ref[...].astype(BF16)
        wb_ref[...] = w
        h = jnp.dot(a, w, preferred_element_type=F32)
        hs[:, 6:8, :] = p_ref[...]
        hs[:, 8:8 + ds, :] = h.reshape(nb, ds, h.shape[-1])
        s_ref[...] = hs[:, 6 + ds:8 + ds, :]
    act = _conv_gate(hs_g, hs_v, 8, ds, cwg_ref, cwv_ref, cbg_ref, cbv_ref, 1)
    act_ref[...] = act.reshape(nb * ds, act.shape[-1]).astype(BF16)


def _ffn_pad(f):
    tk = _down_chunk(f)
    return pl.cdiv(f, tk) * tk


def _ffn_up_sample(h, w_up, conv_w, conv_b, layer, state):
    b, s, d = h.shape
    f = w_up.shape[2] // 2
    tf = _pick(f, 256, 128)
    nf = f // tf
    f_pad = _ffn_pad(f)
    cb3 = conv_b.reshape(conv_b.shape[0], 1, 2 * f)
    real = lambda j: jnp.minimum(j, nf - 1)
    g2 = lambda j: (layer, 0, real(j))
    v2 = lambda j: (layer, 0, nf + real(j))
    st_spec = pl.BlockSpec((b, 2, tf), lambda j: (0, 0, real(j)))
    wb_spec = pl.BlockSpec((d, tf), lambda j: (0, real(j)))
    act, sg, sv, wgb, wvb = pl.pallas_call(
        functools.partial(_up_sample_kernel, nf=nf, nb=b, ds=s),
        grid=(f_pad // tf,),
        in_specs=[pl.BlockSpec((b * s, d), lambda j: (0, 0)),
                  pl.BlockSpec((None, b, 2, tf), lambda j: (layer, 0, 0, real(j))),
                  pl.BlockSpec((None, b, 2, tf), lambda j: (layer, 0, 0, nf + real(j))),
                  pl.BlockSpec((None, d, tf), g2), pl.BlockSpec((None, d, tf), v2),
                  pl.BlockSpec((None, 3, tf), g2), pl.BlockSpec((None, 3, tf), v2),
                  pl.BlockSpec((None, 1, tf), g2), pl.BlockSpec((None, 1, tf), v2)],
        out_specs=[pl.BlockSpec((b * s, tf), lambda j: (0, j)), st_spec, st_spec, wb_spec, wb_spec],
        out_shape=[jax.ShapeDtypeStruct((b * s, f_pad), BF16),
                   jax.ShapeDtypeStruct((b, 2, f), F32), jax.ShapeDtypeStruct((b, 2, f), F32),
                   jax.ShapeDtypeStruct((d, f), BF16), jax.ShapeDtypeStruct((d, f), BF16)],
        scratch_shapes=[pltpu.VMEM((b, 8 + s, tf), F32), pltpu.VMEM((b, 8 + s, tf), F32)],
        compiler_params=_cparams(("arbitrary",), 40),
        name="ffn_up_sample",
    )(h.reshape(b * s, d), state, state, w_up, w_up, conv_w, conv_w, cb3, cb3)
    return act, jnp.concatenate([sg, sv], axis=-1), wgb, wvb


def _ffn_up_prompt(h, wgb, wvb, conv_w, conv_b, layer):
    b, s, d = h.shape
    f = wgb.shape[1]
    f_pad = _ffn_pad(f)
    tf = _pick(f_pad, 512, 128)
    n_sub = 8 if s % 128 == 0 else 1
    last = (f - 1) // tf
    col = lambda bb, j: (0, jnp.minimum(j, last))
    st_spec = pl.BlockSpec((1, 2, tf), lambda bb, j: (bb, 0, jnp.minimum(j, last)))
    cw, cb = conv_w[layer], conv_b[layer].reshape(1, 2 * f)
    act, sg, sv = pl.pallas_call(
        functools.partial(_up_prompt_kernel, f=f, tm=s // n_sub, n_sub=n_sub),
        grid=(b, f_pad // tf),
        in_specs=[pl.BlockSpec((1, s, d), lambda bb, j: (bb, 0, 0), pipeline_mode=pl.Buffered(1)),
                  pl.BlockSpec((d, tf), col), pl.BlockSpec((d, tf), col),
                  pl.BlockSpec((3, tf), col), pl.BlockSpec((3, tf), col),
                  pl.BlockSpec((1, tf), col), pl.BlockSpec((1, tf), col)],
        out_specs=[pl.BlockSpec((1, s, tf), lambda bb, j: (bb, 0, j)), st_spec, st_spec],
        out_shape=[jax.ShapeDtypeStruct((b, s, f_pad), BF16),
                   jax.ShapeDtypeStruct((b, 2, f), F32), jax.ShapeDtypeStruct((b, 2, f), F32)],
        compiler_params=_cparams(("parallel", "arbitrary"), (2 * s * d + 8 * d * tf + 40 * s * tf) // 2 ** 20 + 8),
        name="ffn_up_prompt",
    )(h, wgb, wvb, cw[:, :f], cw[:, f:], cb[:, :f], cb[:, f:])
    return act.reshape(b * s, f_pad), jnp.concatenate([sg, sv], axis=-1)


DOWN_K_CHUNK = 2816


def _down_chunk(f):
    return DOWN_K_CHUNK if f > DOWN_K_CHUNK else f


def _down_kernel(a_ref, w_ref, x_ref, g_ref, o_ref, *rest, kdim, tk, emit, nk):
    k = pl.program_id(2)
    w = w_ref[...]
    if emit:
        w = w.astype(BF16)
        rest[0][...] = w
    if kdim % tk:
        w = jnp.where(lax.broadcasted_iota(jnp.int32, w.shape, 0) < kdim - k * tk, w, jnp.zeros_like(w))
    part = jnp.dot(a_ref[...], w, preferred_element_type=F32)
    if nk == 1:
        o_ref[...] = x_ref[...] + g_ref[...] * part
        return
    acc = rest[-1]

    @pl.when(k == 0)
    def _():
        acc[...] = part

    @pl.when(k > 0)
    def _():
        acc[...] += part

    @pl.when(k == pl.num_programs(2) - 1)
    def _():
        o_ref[...] = x_ref[...] + g_ref[...] * acc[...]


def _ffn_down(act, w, layer, x, mod, k_gate):
    m, f_pad = act.shape
    b, s, d = x.shape
    emit = w.ndim == 3
    f = w.shape[-2]
    tm = _pick(s, 512, 16) if s >= 256 else _pick(m, 512, 16)
    tn = _pick(d, 512, 128)
    tk = _down_chunk(f) if emit else f_pad
    assert not emit or m == tm
    w_spec = (pl.BlockSpec((None, tk, tn), lambda i, j, k: (layer, k, j)) if emit
              else pl.BlockSpec((tk, tn), lambda i, j, k: (k, j)))
    tile = pl.BlockSpec((tm, tn), lambda i, j, k: (i, j))
    extras = _gate_specs(x, mod, k_gate, tm, tn, 3)
    nk = f_pad // tk
    outs = pl.pallas_call(
        functools.partial(_down_kernel, kdim=f, tk=tk, emit=emit, nk=nk),
        grid=(m // tm, d // tn, nk),
        in_specs=[pl.BlockSpec((tm, tk), lambda i, j, k: (i, k)), w_spec] + [sp for _, sp in extras],
        out_specs=[tile] + ([pl.BlockSpec((tk, tn), lambda i, j, k: (k, j))] if emit else []),
        out_shape=[jax.ShapeDtypeStruct((m, d), F32)] + ([jax.ShapeDtypeStruct((f, d), BF16)] if emit else []),
        scratch_shapes=[pltpu.VMEM((tm, tn), F32)] if nk > 1 else [],
        compiler_params=_cparams(("parallel", "arbitrary", "arbitrary"),
                                 (4 * tm * tk + (14 if emit else 4) * tk * tn + 36 * tm * tn) // 2 ** 20 + 6),
        name="ffn_down",
    )(act, w, *[arr for arr, _ in extras])
    out = outs[0].reshape(b, s, d)
    return (out, outs[1]) if emit else out


def kernel(x_prompt, x_sample, c_prompt, c_sample, cache_ckv, cache_kpe, state_pool, state_conv, w_ada, b_ada,
           norm_mix_w, norm_ffn_w, w_dq, q_norm_w, w_uq, w_dkv, kv_norm_w, w_ukv, w_o, w_pool, pool_scale,
           w_up, conv_w, conv_b, w_down, final_norm_w):
    bp, s, d = x_prompt.shape
    bs, ds, _ = x_sample.shape
    past = cache_ckv.shape[2]
    depth = w_ada.shape[0]
    n_heads = w_uq.shape[2]
    q_lora = w_dq.shape[2]
    kv_lora = w_dkv.shape[2] - 64
    qk_head = w_uq.shape[3]
    scale = float(qk_head) ** -0.5
    pos_p = jnp.arange(s, dtype=jnp.int32)
    pos_s = jnp.tile(past + jnp.arange(ds, dtype=jnp.int32), bs)

    nb_all = bp + bs
    pad = (-nb_all) % 16
    c_all = jnp.concatenate([c_prompt, c_sample, jnp.zeros((pad, d), F32)], axis=0)
    mod_all = _ada(c_all, w_ada, b_ada)

    xp, xs = x_prompt, x_sample
    ckv_p_l, kpe_p_l, ckv_s_l, kpe_s_l = [], [], [], []
    pool_p_l, pool_s_l, conv_p_l, conv_s_l = [], [], [], []
    for i in range(depth):
        j = i // 2
        mod_p = mod_all[i, :bp].reshape(bp, 6, 1, d)
        mod_s = mod_all[i, bp:nb_all].reshape(bs, 6, 1, d)
        if i % 2 == 0:
            w_in = jnp.concatenate([w_dq[j], w_dkv[j], jnp.zeros((d, 64), F32)], axis=1).astype(BF16)
            w_q_nope = w_uq[j][:, :, :128].reshape(q_lora, n_heads * 128).astype(BF16)
            w_q_pe = w_uq[j][:, :, 128:].reshape(q_lora, n_heads * 64).astype(BF16)
            w_ukv3 = w_ukv.reshape(w_ukv.shape[0], kv_lora, n_heads * 256)
            up = _normmod(xp, norm_mix_w, i, mod_p, 0, 1, BF16)
            cq, ckv_p, ckv_pb, kpe_p, kpe_pb = _mla_project(up.reshape(bp * s, d), pos_p, w_in, q_norm_w, kv_norm_w, j,
                                                            q_lora, kv_lora)
            qn, qp = _q_heads(cq, w_q_nope, w_q_pe, pos_p, scale)
            kn, vv = _kv_heads(ckv_pb, w_ukv3, j, n_heads)
            r3 = lambda t: t.reshape(bp, s, t.shape[-1])
            o_p = _flash(r3(qn), r3(qp), r3(kn), r3(vv), r3(kpe_pb), n_heads)
            xp = _proj_residual("attn_out_prompt", o_p.reshape(bp * s, n_heads * 128), w_o, j, xp, mod_p, 2)
            us = _normmod(xs, norm_mix_w, i, mod_s, 0, 1, BF16)
            cq, ckv_s, _, kpe_s, _ = _mla_project(us.reshape(bs * ds, d), pos_s, w_in, q_norm_w, kv_norm_w, j,
                                                  q_lora, kv_lora)
            qn, qp = _q_heads(cq, w_q_nope, w_q_pe, pos_s, scale)
            q_lat = _absorb_q(qn, w_ukv3, j, n_heads, bs, ds)
            qp_heads = qp.reshape(bs, ds, n_heads, 64).transpose(0, 2, 1, 3)
            o_lat = _sample_attention(q_lat, qp_heads, cache_ckv, cache_kpe, j, ckv_s.reshape(bs, ds, kv_lora),
                                      kpe_s.reshape(bs, ds, 64))
            o_s = _expand_v(o_lat, w_ukv3, j)
            xs = _proj_residual("attn_out_sample", o_s, w_o, j, xs, mod_s, 2)
            ckv_p_l.append(ckv_p.reshape(bp, s, kv_lora))
            kpe_p_l.append(kpe_p.reshape(bp, s, 64))
            ckv_s_l.append(ckv_s.reshape(bs, ds, kv_lora))
            kpe_s_l.append(kpe_s.reshape(bs, ds, 64))
        else:
            hist = state_pool.shape[2]
            up = _normmod(xp, norm_mix_w, i, mod_p, 0, 1, F32)
            us = _normmod(xs, norm_mix_w, i, mod_s, 0, 1, F32)
            pool_p_l.append(up[:, s - hist:])
            pool_s_l.append(jnp.concatenate([state_pool[j], us], axis=1)[:, -hist:])
            xp = _pool_mix(up, None, w_pool, pool_scale, j, xp, mod_p, 2, 0)
            xs = _pool_mix(us, state_pool, w_pool, pool_scale, j, xs, mod_s, 2, past)
        hp = _normmod(xp, norm_ffn_w, i, mod_p, 3, 4, BF16)
        hs = _normmod(xs, norm_ffn_w, i, mod_s, 3, 4, BF16)
        act_s, cst_s, w_gate_b, w_value_b = _ffn_up_sample(hs, w_up, conv_w, conv_b, i, state_conv)
        act_p, cst_p = _ffn_up_prompt(hp, w_gate_b, w_value_b, conv_w, conv_b, i)
        conv_p_l.append(cst_p)
        conv_s_l.append(cst_s)
        xs, w_down_b = _ffn_down(act_s, w_down, i, xs, mod_s, 5)
        xp = _ffn_down(act_p, w_down_b, i, xp, mod_p, 5)

    y_prompt = _final_norm(xp, final_norm_w)
    y_sample = _final_norm(xs, final_norm_w)
    return (y_prompt, y_sample,
            jnp.stack(ckv_p_l), jnp.stack(kpe_p_l), jnp.stack(ckv_s_l), jnp.stack(kpe_s_l),
            jnp.stack(pool_p_l), jnp.stack(pool_s_l), jnp.stack(conv_p_l), jnp.stack(conv_s_l))
```

```python
import functools

import jax
import jax.numpy as jnp
from jax import lax
from jax.experimental import pallas as pl
from jax.experimental.pallas import tpu as pltpu

F32 = jnp.float32
BF16 = jnp.bfloat16

CHUNK = 64
ROPE_THETA = 10000.0
RMS_EPS = 1e-6
NEG_INF = -1e30
POOL_WINDOWS = (2, 4, 8, 16)
POOL_HALO = 32
V7X_VMEM_CAP_MB = 56


def _cparams(sem, vmem_mb, flags=None):
    return pltpu.CompilerParams(dimension_semantics=sem, vmem_limit_bytes=min(vmem_mb, V7X_VMEM_CAP_MB) * 2 ** 20,
                                flags=flags)


def _pick(dim, pref, align):
    t = min(pref, dim)
    t -= t % align
    while t >= align:
        if dim % t == 0:
            return t
        t -= align
    return dim


def _silu(x):
    return x * (1.0 / (1.0 + jnp.exp(-x)))


def _rms(x, w):
    return x * lax.rsqrt(jnp.mean(x * x, axis=-1, keepdims=True) + RMS_EPS) * w


def _rope_rot(x):
    n = x.shape[-1]
    lane = lax.broadcasted_iota(jnp.int32, x.shape, x.ndim - 1)
    return jnp.where(lane % 64 < 32, -pltpu.roll(x, n - 32, x.ndim - 1), pltpu.roll(x, 32, x.ndim - 1))


def _rope(x, cos_ref, sin_ref):
    reps = x.shape[-1] // 128
    c = cos_ref[...]
    s = sin_ref[...]
    if reps > 1:
        c = jnp.concatenate([c] * reps, axis=-1)
        s = jnp.concatenate([s] * reps, axis=-1)
    return x * c + _rope_rot(x) * s


def _ada_kernel(c_ref, w_ref, b_ref, o_ref):
    a = _silu(c_ref[...]).astype(BF16)
    o_ref[...] = jnp.dot(a, w_ref[...].astype(BF16), preferred_element_type=F32) + b_ref[...]


def _ada(c, w_ada, b_ada):
    n_layers, d, n = w_ada.shape
    bc = c.shape[0]
    tn = _pick(n, 512, 128)
    return pl.pallas_call(
        _ada_kernel,
        grid=(n_layers, n // tn),
        in_specs=[pl.BlockSpec((bc, d), lambda l, j: (0, 0)),
                  pl.BlockSpec((None, d, tn), lambda l, j: (l, 0, j)),
                  pl.BlockSpec((None, 1, tn), lambda l, j: (l, 0, j))],
        out_specs=pl.BlockSpec((None, bc, tn), lambda l, j: (l, 0, j)),
        out_shape=jax.ShapeDtypeStruct((n_layers, bc, n), F32),
        compiler_params=_cparams(("parallel", "parallel"), 4 * d * tn * 3 // 2 ** 20 + 8),
        name="ada",
    )(c, w_ada, b_ada.reshape(n_layers, 1, n))


def _normmod_kernel(x_ref, w_ref, sh_ref, sc_ref, o_ref):
    y = _rms(x_ref[...], w_ref[...])
    o_ref[...] = (y * (1.0 + sc_ref[...]) + sh_ref[...]).astype(o_ref.dtype)


def _norm_kernel(x_ref, w_ref, o_ref):
    o_ref[...] = _rms(x_ref[...], w_ref[...]).astype(o_ref.dtype)


def _row_blocks(b, s):
    if s >= 256:
        return 1, _pick(s, 256, 16)
    return _pick(b, max(1, 256 // s), 1), s


def _normmod(x, w, layer, mod, k_shift, k_scale, out_dtype):
    b, s, d = x.shape
    nb, ts = _row_blocks(b, s)
    w3 = w.reshape(w.shape[0], 1, d)
    return pl.pallas_call(
        _normmod_kernel,
        grid=(b // nb, s // ts),
        in_specs=[pl.BlockSpec((nb, ts, d), lambda i, j: (i, j, 0)),
                  pl.BlockSpec((1, 1, d), lambda i, j: (layer, 0, 0)),
                  pl.BlockSpec((nb, None, 1, d), lambda i, j: (i, k_shift, 0, 0)),
                  pl.BlockSpec((nb, None, 1, d), lambda i, j: (i, k_scale, 0, 0))],
        out_specs=pl.BlockSpec((nb, ts, d), lambda i, j: (i, j, 0)),
        out_shape=jax.ShapeDtypeStruct((b, s, d), out_dtype),
        compiler_params=_cparams(("parallel", "parallel"), 32),
        name="normmod",
    )(x, w3, mod, mod)


def _final_norm(x, w):
    b, s, d = x.shape
    nb, ts = _row_blocks(b, s)
    return pl.pallas_call(
        _norm_kernel,
        grid=(b // nb, s // ts),
        in_specs=[pl.BlockSpec((nb, ts, d), lambda i, j: (i, j, 0)),
                  pl.BlockSpec((1, 1, d), lambda i, j: (0, 0, 0))],
        out_specs=pl.BlockSpec((nb, ts, d), lambda i, j: (i, j, 0)),
        out_shape=jax.ShapeDtypeStruct((b, s, d), F32),
        compiler_params=_cparams(("parallel", "parallel"), 32),
        name="final_norm",
    )(x, w.reshape(1, 1, d))


def _mm(name, grid, a, a_spec, b, b_spec, extras, outs, epilogue, *, trans_b=False, sem, vmem_mb):
    n_ex, n_out = len(extras), len(outs)
    dn = (((1,), (1 if trans_b else 0,)), ((), ()))

    def kern(*refs):
        a_blk = refs[0][...]
        a_blk = a_blk.reshape(-1, a_blk.shape[-1]).astype(BF16)
        acc = lax.dot_general(a_blk, refs[1][...].astype(BF16), dn, preferred_element_type=F32)
        epilogue(acc, refs[2:2 + n_ex], refs[2 + n_ex:2 + n_ex + n_out])

    return pl.pallas_call(
        kern,
        grid=grid,
        in_specs=[a_spec, b_spec] + [s for _, s in extras],
        out_specs=[s for _, s in outs],
        out_shape=[sd for sd, _ in outs],
        compiler_params=_cparams(sem, vmem_mb),
        name=name,
    )(a, b, *[x for x, _ in extras])


def _rope_tables(pos):
    half = 32
    inv = jnp.power(ROPE_THETA, -jnp.arange(half, dtype=F32) / half)
    ang = pos.astype(F32)[:, None] * inv[None, :]
    return jnp.tile(jnp.cos(ang), (1, 4)), jnp.tile(jnp.sin(ang), (1, 4))


def _mla_project(u2, pos_rows, w_in, q_norm_w, kv_norm_w, layer, q_lora, kv_lora):
    m, d = u2.shape
    n = w_in.shape[1]
    period = pos_rows.shape[0]
    tm = _pick(period, 512, 16)
    cos, sin = _rope_tables(pos_rows)
    npb = period // tm

    def epilogue(acc, ex, o):
        qw_ref, kw_ref, cos_ref, sin_ref = ex
        o[0][...] = _rms(acc[:, :q_lora], qw_ref[...]).astype(BF16)
        ckv = _rms(acc[:, q_lora:q_lora + kv_lora], kw_ref[...])
        o[1][...] = ckv
        o[2][...] = ckv.astype(BF16)
        kpe = _rope(acc[:, q_lora + kv_lora:], cos_ref, sin_ref)[:, :64]
        o[3][...] = kpe
        o[4][...] = kpe.astype(BF16)

    row = lambda width: pl.BlockSpec((tm, width), lambda i: (i, 0))
    return _mm(
        "mla_in", (m // tm,), u2, pl.BlockSpec((tm, d), lambda i: (i, 0)),
        w_in, pl.BlockSpec((d, n), lambda i: (0, 0)),
        [(q_norm_w.reshape(-1, 1, q_lora), pl.BlockSpec((None, 1, q_lora), lambda i: (layer, 0, 0))),
         (kv_norm_w.reshape(-1, 1, kv_lora), pl.BlockSpec((None, 1, kv_lora), lambda i: (layer, 0, 0))),
         (cos, pl.BlockSpec((tm, 128), lambda i: (i % npb, 0))),
         (sin, pl.BlockSpec((tm, 128), lambda i: (i % npb, 0)))],
        [(jax.ShapeDtypeStruct((m, q_lora), BF16), row(q_lora)),
         (jax.ShapeDtypeStruct((m, kv_lora), F32), row(kv_lora)),
         (jax.ShapeDtypeStruct((m, kv_lora), BF16), row(kv_lora)),
         (jax.ShapeDtypeStruct((m, 64), F32), row(64)),
         (jax.ShapeDtypeStruct((m, 64), BF16), row(64))],
        epilogue, sem=("parallel",), vmem_mb=(4 * d * n + 4 * tm * d + 16 * tm * n) // 2 ** 20 + 8)


def _q_heads(cq, w_nope, w_pe, pos_rows, scale):
    m, kq = cq.shape
    period = pos_rows.shape[0]
    tm = _pick(period, 512, 16)
    npb = period // tm
    cos, sin = _rope_tables(pos_rows)

    def plain(acc, ex, o):
        o[0][...] = (acc * scale).astype(BF16)

    def roped(acc, ex, o):
        o[0][...] = (_rope(acc, ex[0], ex[1]) * scale).astype(BF16)

    outs = []
    for name, w, extras, epi in (("q_nope", w_nope, [], plain),
                                 ("q_pe", w_pe, [(cos, pl.BlockSpec((tm, 128), lambda i, j: (i % npb, 0))),
                                                 (sin, pl.BlockSpec((tm, 128), lambda i, j: (i % npb, 0)))], roped)):
        n = w.shape[1]
        tn = _pick(n, 2048, 128)
        outs.append(_mm(name, (m // tm, n // tn), cq, pl.BlockSpec((tm, kq), lambda i, j: (i, 0)),
                        w, pl.BlockSpec((kq, tn), lambda i, j: (0, j)), extras,
                        [(jax.ShapeDtypeStruct((m, n), BF16), pl.BlockSpec((tm, tn), lambda i, j: (i, j)))],
                        epi, sem=("parallel", "parallel"), vmem_mb=40)[0])
    return outs


def _kv_heads(ckv_b, w_ukv3, layer, n_heads):
    m, kc = ckv_b.shape
    tm = _pick(m, 512, 16)
    hg = 4 if n_heads % 4 == 0 else 1

    def kern(c_ref, w_ref, k_ref, v_ref):
        c = c_ref[...]
        for g in range(n_heads // hg):
            acc = jnp.dot(c, w_ref[:, g * hg * 256:(g + 1) * hg * 256].astype(BF16), preferred_element_type=F32)
            for h in range(hg):
                cols = slice((g * hg + h) * 128, (g * hg + h + 1) * 128)
                k_ref[:, cols] = acc[:, h * 256:h * 256 + 128].astype(BF16)
                v_ref[:, cols] = acc[:, h * 256 + 128:(h + 1) * 256].astype(BF16)

    out = jax.ShapeDtypeStruct((m, n_heads * 128), BF16)
    out_spec = pl.BlockSpec((tm, n_heads * 128), lambda i: (i, 0))
    return pl.pallas_call(
        kern,
        grid=(m // tm,),
        in_specs=[pl.BlockSpec((tm, kc), lambda i: (i, 0)),
                  pl.BlockSpec((None, kc, n_heads * 256), lambda i: (layer, 0, 0), pipeline_mode=pl.Buffered(1))],
        out_specs=[out_spec, out_spec],
        out_shape=[out, out],
        compiler_params=_cparams(("parallel",), (4 * kc * n_heads * 256 + 8 * tm * n_heads * 128) // 2 ** 20 + 16),
        name="kv_heads",
    )(ckv_b, w_ukv3)


def _gate_specs(x, mod, k_gate, tm, tn):
    b, s, d = x.shape
    x2 = x.reshape(b * s, d)
    x_spec = pl.BlockSpec((tm, tn), lambda i, j: (i, j))
    if s % tm == 0:
        per = s // tm
        g = (mod, pl.BlockSpec((None, None, 1, tn), lambda i, j: (i // per, k_gate, 0, j)))
    else:
        rows = jnp.repeat(mod[:, k_gate, 0, :], s, axis=0)
        g = (rows, pl.BlockSpec((tm, tn), lambda i, j: (i, j)))
    return [(x2, x_spec), g]


LHS_TILE_BYTES = 12 * 2 ** 20


def _proj_residual(name, a2, w, layer, x, mod, k_gate):
    m, kk = a2.shape
    n = w.shape[-1]
    b, s, _ = x.shape
    rows = min(1024, LHS_TILE_BYTES // (2 * kk))
    tm = _pick(s, rows, 16) if s >= 256 else _pick(m, rows, 16)
    tn = _pick(n, 512, 128)
    w_spec = (pl.BlockSpec((None, kk, tn), lambda i, j: (layer, 0, j)) if w.ndim == 3
              else pl.BlockSpec((kk, tn), lambda i, j: (0, j)))

    def epilogue(acc, ex, o):
        o[0][...] = ex[0][...] + ex[1][...] * acc

    out = _mm(name, (m // tm, n // tn), a2, pl.BlockSpec((tm, kk), lambda i, j: (i, 0)),
              w, w_spec, _gate_specs(x, mod, k_gate, tm, tn),
              [(jax.ShapeDtypeStruct((m, n), F32), pl.BlockSpec((tm, tn), lambda i, j: (i, j)))],
              epilogue, sem=("parallel", "arbitrary"),
              vmem_mb=(4 * tm * kk + (10 if w.ndim == 3 else 4) * kk * tn + 36 * tm * tn) // 2 ** 20 + 8)[0]
    return out.reshape(b, s, n)


def _flash_kernel(qn_ref, qp_ref, kn_ref, v_ref, kpe_ref, wg_ref, wv_ref, o_ref, wgb_ref, wvb_ref, *, t, nq, n_cast):
    @pl.when(pl.program_id(0) * pl.num_programs(1) + pl.program_id(1) < n_cast)
    def _():
        wgb_ref[...] = wg_ref[...].astype(BF16)
        wvb_ref[...] = wv_ref[...].astype(BF16)

    nt = (((1,), (1,)), ((), ()))
    kpe = kpe_ref[0]
    qc = lax.broadcasted_iota(jnp.int32, (t, t), 0) // CHUNK
    kc = lax.broadcasted_iota(jnp.int32, (t, t), 1) // CHUNK
    diag_visible = kc <= qc
    for hh in range(2):
        hs = slice(hh * 128, (hh + 1) * 128)
        kcat = jnp.concatenate([kn_ref[0, :, hs], kpe], axis=-1)
        for qi in range(nq):
            lo = qi * t
            rows = slice(lo, lo + t)
            q = jnp.concatenate([qn_ref[0, rows, hs], qp_ref[0, rows, hh * 64:(hh + 1) * 64]], axis=-1)
            s_d = jnp.where(diag_visible, lax.dot_general(q, kcat[lo:lo + t], nt, preferred_element_type=F32), NEG_INF)
            m = jnp.max(s_d, axis=-1, keepdims=True)
            if qi > 0:
                s_o = lax.dot_general(q, kcat[:lo], nt, preferred_element_type=F32)
                m = jnp.maximum(m, jnp.max(s_o, axis=-1, keepdims=True))
                p_o = jnp.exp(s_o - m)
            p_d = jnp.exp(s_d - m)
            l = jnp.sum(p_d, axis=-1, keepdims=True)
            acc = jnp.dot(p_d.astype(BF16), v_ref[0, rows, hs], preferred_element_type=F32)
            if qi > 0:
                l = l + jnp.sum(p_o, axis=-1, keepdims=True)
                acc = acc + jnp.dot(p_o.astype(BF16), v_ref[0, 0:lo, hs], preferred_element_type=F32)
            o_ref[0, rows, hs] = (acc / l).astype(BF16)


def _flash(qn, qp, kn, v, kpe, n_heads, w_up, layer):
    b, s, _ = qn.shape
    d, f = w_up.shape[1], w_up.shape[2] // 2
    t = _pick(s, 256, CHUNK)
    tf = _pick(f, 256, 128)
    n_cast = f // tf
    hp_steps = n_heads // 2
    assert n_cast <= b * hp_steps
    cast_tile = lambda bb, hp: jnp.minimum(bb * hp_steps + hp, n_cast - 1)
    kern = functools.partial(_flash_kernel, t=t, nq=s // t, n_cast=n_cast)
    wide = pl.BlockSpec((1, s, 256), lambda bb, hp: (bb, 0, hp))
    wb_spec = pl.BlockSpec((d, tf), lambda bb, hp: (0, cast_tile(bb, hp)))
    return pl.pallas_call(
        kern,
        grid=(b, hp_steps),
        in_specs=[wide, pl.BlockSpec((1, s, 128), lambda bb, hp: (bb, 0, hp)), wide, wide,
                  pl.BlockSpec((1, s, 64), lambda bb, hp: (bb, 0, 0)),
                  pl.BlockSpec((None, d, tf), lambda bb, hp: (layer, 0, cast_tile(bb, hp))),
                  pl.BlockSpec((None, d, tf), lambda bb, hp: (layer, 0, n_cast + cast_tile(bb, hp)))],
        out_specs=[wide, wb_spec, wb_spec],
        out_shape=[jax.ShapeDtypeStruct((b, s, n_heads * 128), BF16),
                   jax.ShapeDtypeStruct((d, f), BF16), jax.ShapeDtypeStruct((d, f), BF16)],
        compiler_params=_cparams(("arbitrary", "arbitrary"), 30 + 24 * d * tf // 2 ** 20),
        name="flash_prompt",
    )(qn, qp, kn, v, kpe, w_up, w_up)


def _sattn_kernel(ql_ref, qp_ref, ck_ref, kp_ref, nck_ref, nkp_ref, o_ref, *, past, n_split):
    nt = (((1,), (1,)), ((), ()))
    n_heads, ds, c = ql_ref.shape[1:]
    ck = ck_ref[0].astype(BF16)
    kp = kp_ref[0].astype(BF16)
    nck = nck_ref[0].astype(BF16)
    nkp = nkp_ref[0].astype(BF16)
    hg = n_heads // n_split
    rows = hg * ds
    qpos = past + lax.broadcasted_iota(jnp.int32, (rows, ds), 0) % ds
    kpos = past + lax.broadcasted_iota(jnp.int32, (rows, ds), 1)
    new_visible = kpos // CHUNK <= qpos // CHUNK
    for g in range(n_split):
        ql = ql_ref[0, g * hg:(g + 1) * hg].reshape(rows, c)
        qp = qp_ref[0, g * hg:(g + 1) * hg].reshape(rows, qp_ref.shape[-1])
        s_c = lax.dot_general(ql, ck, nt, preferred_element_type=F32) + lax.dot_general(qp, kp, nt, preferred_element_type=F32)
        s_n = lax.dot_general(ql, nck, nt, preferred_element_type=F32) + lax.dot_general(qp, nkp, nt, preferred_element_type=F32)
        s_n = jnp.where(new_visible, s_n, NEG_INF)
        m = jnp.maximum(jnp.max(s_c, axis=-1, keepdims=True), jnp.max(s_n, axis=-1, keepdims=True))
        p_c = jnp.exp(s_c - m)
        p_n = jnp.exp(s_n - m)
        l = jnp.sum(p_c, axis=-1, keepdims=True) + jnp.sum(p_n, axis=-1, keepdims=True)
        acc = (jnp.dot(p_c.astype(BF16), ck, preferred_element_type=F32)
               + jnp.dot(p_n.astype(BF16), nck, preferred_element_type=F32))
        o_ref[0, g * hg:(g + 1) * hg] = (acc / l).astype(BF16).reshape(hg, ds, c)


def _sample_attention(q_lat, q_pe, cache_ckv, cache_kpe, layer, ckv_new, kpe_new):
    b, n_heads, ds, c = q_lat.shape
    past = cache_ckv.shape[2]
    n_split = 2 if n_heads % 2 == 0 else 1
    kern = functools.partial(_sattn_kernel, past=past, n_split=n_split)
    per_stream = lambda *blk: pl.BlockSpec((1,) + blk, lambda bb: (bb,) + (0,) * len(blk))
    rows = n_heads * ds // n_split
    return pl.pallas_call(
        kern,
        grid=(b,),
        in_specs=[per_stream(n_heads, ds, c), per_stream(n_heads, ds, 64),
                  pl.BlockSpec((None, 1, past, c), lambda bb: (layer, bb, 0, 0)),
                  pl.BlockSpec((None, 1, past, 64), lambda bb: (layer, bb, 0, 0)),
                  per_stream(ds, c), per_stream(ds, 64)],
        out_specs=per_stream(n_heads, ds, c),
        out_shape=jax.ShapeDtypeStruct((b, n_heads, ds, c), BF16),
        compiler_params=_cparams(("parallel",), (2 * 4 * past * (c + 128) + 2 * past * c + 12 * rows * past) // 2 ** 20 + 10),
        name="attn_sample",
    )(q_lat, q_pe, cache_ckv, cache_kpe, ckv_new, kpe_new)


def _absorb_q(qn, w_ukv3, layer, n_heads, b, ds):
    m = qn.shape[0]
    c = w_ukv3.shape[1]

    def epilogue(acc, ex, o):
        o[0][...] = acc.astype(BF16).reshape(b, ds, c)

    return _mm("absorb_q", (n_heads,), qn, pl.BlockSpec((m, 128), lambda h: (0, h)),
               w_ukv3, pl.BlockSpec((None, c, 128), lambda h: (layer, 0, 2 * h)), [],
               [(jax.ShapeDtypeStruct((b, n_heads, ds, c), BF16), pl.BlockSpec((b, None, ds, c), lambda h: (0, h, 0, 0)))],
               epilogue, trans_b=True, sem=("parallel",), vmem_mb=16)[0]


def _expand_v(o_lat, w_ukv3, layer):
    b, n_heads, ds, c = o_lat.shape
    m = b * ds

    def epilogue(acc, ex, o):
        o[0][...] = acc.astype(BF16)

    return _mm("expand_v", (n_heads,), o_lat, pl.BlockSpec((b, None, ds, c), lambda h: (0, h, 0, 0)),
               w_ukv3, pl.BlockSpec((None, c, 128), lambda h: (layer, 0, 2 * h + 1)), [],
               [(jax.ShapeDtypeStruct((m, n_heads * 128), BF16), pl.BlockSpec((m, 128), lambda h: (0, h)))],
               epilogue, sem=("parallel",), vmem_mb=16)[0]


def _pool_kernel(u_ref, prev_ref, w_ref, ps_ref, x_ref, g_ref, o_ref, st, *, ts, pos0, prev_rows, seq_tiles):
    sidx = pl.program_id(1)
    grp = pl.program_id(2)
    nb, gw = u_ref.shape[0], u_ref.shape[2]
    total = POOL_HALO + ts
    u = u_ref[...]
    if prev_rows == POOL_HALO:
        halo = prev_ref[...]
        st[:, 0:POOL_HALO, :] = jnp.where(sidx % seq_tiles == 0, jnp.zeros_like(halo), halo)
    else:
        st[:, 0:POOL_HALO, :] = jnp.zeros((nb, POOL_HALO, gw), F32)
        st[:, POOL_HALO - prev_rows:POOL_HALO, :] = prev_ref[...]
    st[:, POOL_HALO:total, :] = u
    s = st[...].reshape(nb * total, gw)
    sums = []
    for k in range(len(POOL_WINDOWS)):
        s = s + pltpu.roll(s, 2 ** k, 0)
        sums.append(s)
    win = sums[-1]
    for k in range(len(POOL_WINDOWS) - 2, -1, -1):
        win = jnp.where(grp == k, sums[k], win)
    win = win.reshape(nb, total, gw)[:, POOL_HALO:, :]
    wsize = jnp.left_shift(2, grp)
    pos = pos0 + (sidx % seq_tiles) * ts + lax.broadcasted_iota(jnp.int32, (nb, ts, gw), 1)
    cnt = jnp.minimum(pos + 1, wsize).astype(F32)
    d = (win / cnt - u).astype(BF16).reshape(nb * ts, gw)
    y = jnp.dot(d, w_ref[...].astype(BF16), preferred_element_type=F32).reshape(nb, ts, gw)
    o_ref[...] = x_ref[...] + g_ref[...] * (y * ps_ref[...])


def _pool_mix(u, prev, w_pool, pool_scale, layer_j, x, mod, k_gate, pos0):
    b, s, d = u.shape
    ng = w_pool.shape[1]
    gw = d // ng
    nb, ts = _row_blocks(b, s)
    if prev is None:
        nb, ts = 1, _pick(s, 512, POOL_HALO)
        seq_tiles = s // ts
        per = ts // POOL_HALO
        prev_arr, prev_rows = u, POOL_HALO
        prev_spec = pl.BlockSpec((1, POOL_HALO, gw), lambda i, j, g: (i, jnp.maximum(j * per - 1, 0), g))
    else:
        nb = _pick(b, 8, 1)
        seq_tiles = 1
        prev_rows = prev.shape[2]
        prev_arr = prev
        prev_spec = pl.BlockSpec((None, nb, prev_rows, gw), lambda i, j, g: (layer_j, i, 0, g))
    kern = functools.partial(_pool_kernel, ts=ts, pos0=pos0, prev_rows=prev_rows, seq_tiles=seq_tiles)
    blk = pl.BlockSpec((nb, ts, gw), lambda i, j, g: (i, j, g))
    return pl.pallas_call(
        kern,
        grid=(b // nb, s // ts, ng),
        in_specs=[blk, prev_spec,
                  pl.BlockSpec((None, None, gw, gw), lambda i, j, g: (layer_j, g, 0, 0)),
                  pl.BlockSpec((None, 1, 1, gw), lambda i, j, g: (layer_j, 0, 0, g)),
                  blk,
                  pl.BlockSpec((nb, None, 1, gw), lambda i, j, g: (i, k_gate, 0, g))],
        out_specs=blk,
        out_shape=jax.ShapeDtypeStruct((b, s, d), F32),
        scratch_shapes=[pltpu.VMEM((nb, POOL_HALO + ts, gw), F32)],
        compiler_params=_cparams(("parallel", "arbitrary", "arbitrary"), 48),
        name="pool_mix",
    )(u, prev_arr, w_pool, pool_scale.reshape(pool_scale.shape[0], 1, 1, d), x, mod)


def _conv_gate(hs_g, hs_v, base, rows, cw_g, cw_v, cb_g, cb_v, seq_axis):
    def conv(hs, cw, cb):
        def sl(off):
            idx = [slice(None)] * len(hs.shape)
            idx[seq_axis] = slice(base - off, base - off + rows)
            return hs[tuple(idx)]
        return cb[...] + sl(2) * cw[0:1, :] + sl(1) * cw[1:2, :] + sl(0) * cw[2:3, :]
    return _silu(conv(hs_g, cw_g, cb_g)) * conv(hs_v, cw_v, cb_v)


def _up_prompt_kernel(u_ref, wg_ref, wv_ref, cwg_ref, cwv_ref, cbg_ref, cbv_ref, wd_ref,
                      act_ref, sg_ref, sv_ref, wdb_ref, *, tm, n_sub):
    wdb_ref[...] = wd_ref[...].astype(BF16)
    tf = wg_ref.shape[1]
    streams = ((wg_ref, cwg_ref, cbg_ref), (wv_ref, cwv_ref, cbv_ref))
    sub = lax.broadcasted_iota(jnp.int32, (8, tf), 0)
    tails = [[jnp.zeros((8, tf), F32)] * 2 for _ in streams]
    for r in range(n_sub):
        a = u_ref[0, r * tm:(r + 1) * tm, :]
        conv = []
        for si, (w_ref, cw, cb) in enumerate(streams):
            h = jnp.dot(a, w_ref[...], preferred_element_type=F32)
            acc = cb[...] + h * cw[2:3, :]
            for k in (1, 2):
                rk = pltpu.roll(h, k, 0)
                hk = jnp.concatenate([jnp.where(sub < k, tails[si][k - 1], rk[:8]), rk[8:]], axis=0)
                tails[si][k - 1] = rk[:8]
                acc = acc + hk * cw[2 - k:3 - k, :]
            conv.append(acc)
        act_ref[0, r * tm:(r + 1) * tm, :] = (_silu(conv[0]) * conv[1]).astype(BF16)
    sg_ref[0] = tails[0][1][0:2]
    sv_ref[0] = tails[1][1][0:2]


def _up_sample_kernel(u_ref, pg_ref, pv_ref, wg_ref, wv_ref, cwg_ref, cwv_ref, cbg_ref, cbv_ref,
                      act_ref, sg_ref, sv_ref, *rest, nb, ds, emit):
    hs_g, hs_v = rest[-2:]
    a = u_ref[...]
    for si, (w_ref, p_ref, hs, s_ref) in enumerate(((wg_ref, pg_ref, hs_g, sg_ref), (wv_ref, pv_ref, hs_v, sv_ref))):
        w = w_ref[...]
        if emit:
            w = w.astype(BF16)
            rest[si][...] = w
        h = jnp.dot(a, w, preferred_element_type=F32)
        hs[:, 6:8, :] = p_ref[...]
        hs[:, 8:8 + ds, :] = h.reshape(nb, ds, h.shape[-1])
        s_ref[...] = hs[:, 6 + ds:8 + ds, :]
    act = _conv_gate(hs_g, hs_v, 8, ds, cwg_ref, cwv_ref, cbg_ref, cbv_ref, 1)
    act_ref[...] = act.reshape(nb * ds, act.shape[-1]).astype(BF16)


def _ffn_up_sample(h, w_gate, w_value, conv_w, conv_b, layer, state):
    b, s, d = h.shape
    emit = w_gate.ndim == 3
    f = w_gate.shape[-1] // 2 if emit else w_gate.shape[-1]
    tf = _pick(f, 256, 128)
    nf = f // tf
    cb3 = conv_b.reshape(conv_b.shape[0], 1, 2 * f)
    g2 = lambda j: (layer, 0, j)
    v2 = lambda j: (layer, 0, nf + j)
    col = lambda j: (0, j)
    st_spec = pl.BlockSpec((b, 2, tf), lambda j: (0, 0, j))
    w_specs = ([pl.BlockSpec((None, d, tf), g2), pl.BlockSpec((None, d, tf), v2)] if emit
               else [pl.BlockSpec((d, tf), col), pl.BlockSpec((d, tf), col)])
    copies = [jax.ShapeDtypeStruct((d, f), BF16)] * 2 if emit else []
    outs = pl.pallas_call(
        functools.partial(_up_sample_kernel, nb=b, ds=s, emit=emit),
        grid=(nf,),
        in_specs=[pl.BlockSpec((b * s, d), lambda j: (0, 0)),
                  pl.BlockSpec((None, b, 2, tf), lambda j: (layer, 0, 0, j)),
                  pl.BlockSpec((None, b, 2, tf), lambda j: (layer, 0, 0, nf + j))]
                 + w_specs
                 + [pl.BlockSpec((None, 3, tf), g2), pl.BlockSpec((None, 3, tf), v2),
                    pl.BlockSpec((None, 1, tf), g2), pl.BlockSpec((None, 1, tf), v2)],
        out_specs=[pl.BlockSpec((b * s, tf), col), st_spec, st_spec] + [pl.BlockSpec((d, tf), col)] * len(copies),
        out_shape=[jax.ShapeDtypeStruct((b * s, f), BF16),
                   jax.ShapeDtypeStruct((b, 2, f), F32), jax.ShapeDtypeStruct((b, 2, f), F32)] + copies,
        scratch_shapes=[pltpu.VMEM((b, 8 + s, tf), F32), pltpu.VMEM((b, 8 + s, tf), F32)],
        compiler_params=_cparams(("arbitrary",), 40),
        name="ffn_up_sample",
    )(h.reshape(b * s, d), state, state, w_gate, w_value, conv_w, conv_w, cb3, cb3)
    return (outs[0], jnp.concatenate(outs[1:3], axis=-1)) + tuple(outs[3:])


def _ffn_up_prompt(h, wgb, wvb, conv_w, conv_b, w_down, layer):
    b, s, d = h.shape
    f = wgb.shape[1]
    tf = min(f, 512)
    n_col = pl.cdiv(f, tf)
    n_sub = 4 if s % 64 == 0 else 1
    tr = -(-pl.cdiv(f, n_col) // 16) * 16
    tc = d // b
    assert tc % 128 == 0 and (n_col - 1) * tr < f <= n_col * tr
    col = lambda bb, j: (0, j)
    st_spec = pl.BlockSpec((1, 2, tf), lambda bb, j: (bb, 0, j))
    cw, cb = conv_w[layer], conv_b[layer].reshape(1, 2 * f)
    act, sg, sv, w_down_b = pl.pallas_call(
        functools.partial(_up_prompt_kernel, tm=s // n_sub, n_sub=n_sub),
        grid=(b, n_col),
        in_specs=[pl.BlockSpec((1, s, d), lambda bb, j: (bb, 0, 0), pipeline_mode=pl.Buffered(1)),
                  pl.BlockSpec((d, tf), col), pl.BlockSpec((d, tf), col),
                  pl.BlockSpec((3, tf), col), pl.BlockSpec((3, tf), col),
                  pl.BlockSpec((1, tf), col), pl.BlockSpec((1, tf), col),
                  pl.BlockSpec((None, tr, tc), lambda bb, j: (layer, j, bb))],
        out_specs=[pl.BlockSpec((1, s, tf), lambda bb, j: (bb, 0, j)), st_spec, st_spec,
                   pl.BlockSpec((tr, tc), lambda bb, j: (j, bb))],
        out_shape=[jax.ShapeDtypeStruct((b, s, f), BF16),
                   jax.ShapeDtypeStruct((b, 2, f), F32), jax.ShapeDtypeStruct((b, 2, f), F32),
                   jax.ShapeDtypeStruct((f, d), BF16)],
        compiler_params=_cparams(("arbitrary", "arbitrary"),
                                 (2 * s * d + 8 * d * tf + 40 * s * tf + 12 * tr * tc) // 2 ** 20 + 8),
        name="ffn_up_prompt",
    )(h, wgb, wvb, cw[:, :f], cw[:, f:], cb[:, :f], cb[:, f:], w_down)
    return act.reshape(b * s, f), jnp.concatenate([sg, sv], axis=-1), w_down_b


def kernel(x_prompt, x_sample, c_prompt, c_sample, cache_ckv, cache_kpe, state_pool, state_conv, w_ada, b_ada,
           norm_mix_w, norm_ffn_w, w_dq, q_norm_w, w_uq, w_dkv, kv_norm_w, w_ukv, w_o, w_pool, pool_scale,
           w_up, conv_w, conv_b, w_down, final_norm_w):
    bp, s, d = x_prompt.shape
    bs, ds, _ = x_sample.shape
    past = cache_ckv.shape[2]
    depth = w_ada.shape[0]
    n_heads = w_uq.shape[2]
    q_lora = w_dq.shape[2]
    kv_lora = w_dkv.shape[2] - 64
    qk_head = w_uq.shape[3]
    scale = float(qk_head) ** -0.5
    pos_p = jnp.arange(s, dtype=jnp.int32)
    pos_s = jnp.tile(past + jnp.arange(ds, dtype=jnp.int32), bs)

    nb_all = bp + bs
    pad = (-nb_all) % 16
    c_all = jnp.concatenate([c_prompt, c_sample, jnp.zeros((pad, d), F32)], axis=0)
    mod_all = _ada(c_all, w_ada, b_ada)

    xp, xs = x_prompt, x_sample
    ckv_p_l, kpe_p_l, ckv_s_l, kpe_s_l = [], [], [], []
    pool_p_l, pool_s_l, conv_p_l, conv_s_l = [], [], [], []
    for i in range(depth):
        j = i // 2
        mod_p = mod_all[i, :bp].reshape(bp, 6, 1, d)
        mod_s = mod_all[i, bp:nb_all].reshape(bs, 6, 1, d)
        w_up_b = None
        if i % 2 == 0:
            w_in = jnp.concatenate([w_dq[j], w_dkv[j], jnp.zeros((d, 64), F32)], axis=1).astype(BF16)
            w_q_nope = w_uq[j][:, :, :128].reshape(q_lora, n_heads * 128).astype(BF16)
            w_q_pe = w_uq[j][:, :, 128:].reshape(q_lora, n_heads * 64).astype(BF16)
            w_ukv3 = w_ukv.reshape(w_ukv.shape[0], kv_lora, n_heads * 256)
            up = _normmod(xp, norm_mix_w, i, mod_p, 0, 1, BF16)
            cq, ckv_p, ckv_pb, kpe_p, kpe_pb = _mla_project(up.reshape(bp * s, d), pos_p, w_in, q_norm_w, kv_norm_w, j,
                                                            q_lora, kv_lora)
            qn, qp = _q_heads(cq, w_q_nope, w_q_pe, pos_p, scale)
            kn, vv = _kv_heads(ckv_pb, w_ukv3, j, n_heads)
            r3 = lambda t: t.reshape(bp, s, t.shape[-1])
            o_p, *w_up_b = _flash(r3(qn), r3(qp), r3(kn), r3(vv), r3(kpe_pb), n_heads, w_up, i)
            xp = _proj_residual("attn_out_prompt", o_p.reshape(bp * s, n_heads * 128), w_o, j, xp, mod_p, 2)
            us = _normmod(xs, norm_mix_w, i, mod_s, 0, 1, BF16)
            cq, ckv_s, _, kpe_s, _ = _mla_project(us.reshape(bs * ds, d), pos_s, w_in, q_norm_w, kv_norm_w, j,
                                                  q_lora, kv_lora)
            qn, qp = _q_heads(cq, w_q_nope, w_q_pe, pos_s, scale)
            q_lat = _absorb_q(qn, w_ukv3, j, n_heads, bs, ds)
            qp_heads = qp.reshape(bs, ds, n_heads, 64).transpose(0, 2, 1, 3)
            o_lat = _sample_attention(q_lat, qp_heads, cache_ckv, cache_kpe, j, ckv_s.reshape(bs, ds, kv_lora),
                                      kpe_s.reshape(bs, ds, 64))
            o_s = _expand_v(o_lat, w_ukv3, j)
            xs = _proj_residual("attn_out_sample", o_s, w_o, j, xs, mod_s, 2)
            ckv_p_l.append(ckv_p.reshape(bp, s, kv_lora))
            kpe_p_l.append(kpe_p.reshape(bp, s, 64))
            ckv_s_l.append(ckv_s.reshape(bs, ds, kv_lora))
            kpe_s_l.append(kpe_s.reshape(bs, ds, 64))
        else:
            hist = state_pool.shape[2]
            up = _normmod(xp, norm_mix_w, i, mod_p, 0, 1, F32)
            us = _normmod(xs, norm_mix_w, i, mod_s, 0, 1, F32)
            pool_p_l.append(up[:, s - hist:])
            pool_s_l.append(jnp.concatenate([state_pool[j], us], axis=1)[:, -hist:])
            xp = _pool_mix(up, None, w_pool, pool_scale, j, xp, mod_p, 2, 0)
            xs = _pool_mix(us, state_pool, w_pool, pool_scale, j, xs, mod_s, 2, past)
        hp = _normmod(xp, norm_ffn_w, i, mod_p, 3, 4, BF16)
        hs = _normmod(xs, norm_ffn_w, i, mod_s, 3, 4, BF16)
        if w_up_b is None:
            act_s, cst_s, *w_up_b = _ffn_up_sample(hs, w_up, w_up, conv_w, conv_b, i, state_conv)
        else:
            act_s, cst_s = _ffn_up_sample(hs, *w_up_b, conv_w, conv_b, i, state_conv)
        act_p, cst_p, w_down_b = _ffn_up_prompt(hp, *w_up_b, conv_w, conv_b, w_down, i)
        conv_p_l.append(cst_p)
        conv_s_l.append(cst_s)
        xs = _proj_residual("ffn_down_sample", act_s, w_down_b, None, xs, mod_s, 5)
        xp = _proj_residual("ffn_down_prompt", act_p, w_down_b, None, xp, mod_p, 5)

    y_prompt = _final_norm(xp, final_norm_w)
    y_sample = _final_norm(xs, final_norm_w)
    return (y_prompt, y_sample,
            jnp.stack(ckv_p_l), jnp.stack(kpe_p_l), jnp.stack(ckv_s_l), jnp.stack(kpe_s_l),
            jnp.stack(pool_p_l), jnp.stack(pool_s_l), jnp.stack(conv_p_l), jnp.stack(conv_s_l))
```

```python
import functools

import jax
import jax.numpy as jnp
from jax import lax
from jax.experimental import pallas as pl
from jax.experimental.pallas import tpu as pltpu

F32 = jnp.float32
BF16 = jnp.bfloat16

CHUNK = 64
ROPE_THETA = 10000.0
RMS_EPS = 1e-6
NEG_INF = -1e30
POOL_WINDOWS = (2, 4, 8, 16)
POOL_HALO = 32
V7X_VMEM_CAP_MB = 56


def _cparams(sem, vmem_mb, flags=None):
    return pltpu.CompilerParams(dimension_semantics=sem, vmem_limit_bytes=min(vmem_mb, V7X_VMEM_CAP_MB) * 2 ** 20,
                                flags=flags)


def _pick(dim, pref, align):
    t = min(pref, dim)
    t -= t % align
    while t >= align:
        if dim % t == 0:
            return t
        t -= align
    return dim


def _silu(x):
    return x * (0.5 + 0.5 * jnp.tanh(0.5 * x))


def _rms(x, w):
    return x * lax.rsqrt(jnp.mean(x * x, axis=-1, keepdims=True) + RMS_EPS) * w


def _rope_rot(x):
    n = x.shape[-1]
    lane = lax.broadcasted_iota(jnp.int32, x.shape, x.ndim - 1)
    return jnp.where(lane % 64 < 32, -pltpu.roll(x, n - 32, x.ndim - 1), pltpu.roll(x, 32, x.ndim - 1))


def _rope(x, cos_ref, sin_ref):
    reps = x.shape[-1] // 128
    c = cos_ref[...]
    s = sin_ref[...]
    if reps > 1:
        c = jnp.concatenate([c] * reps, axis=-1)
        s = jnp.concatenate([s] * reps, axis=-1)
    return x * c + _rope_rot(x) * s


def _ada_kernel(c_ref, w_ref, b_ref, o_ref):
    a = _silu(c_ref[...]).astype(BF16)
    o_ref[...] = jnp.dot(a, w_ref[...].astype(BF16), preferred_element_type=F32) + b_ref[...]


def _ada(c, w_ada, b_ada):
    n_layers, d, n = w_ada.shape
    bc = c.shape[0]
    tn = _pick(n, 512, 128)
    return pl.pallas_call(
        _ada_kernel,
        grid=(n_layers, n // tn),
        in_specs=[pl.BlockSpec((bc, d), lambda l, j: (0, 0)),
                  pl.BlockSpec((None, d, tn), lambda l, j: (l, 0, j)),
                  pl.BlockSpec((None, 1, tn), lambda l, j: (l, 0, j))],
        out_specs=pl.BlockSpec((None, bc, tn), lambda l, j: (l, 0, j)),
        out_shape=jax.ShapeDtypeStruct((n_layers, bc, n), F32),
        compiler_params=_cparams(("parallel", "parallel"), 4 * d * tn * 3 // 2 ** 20 + 8),
        name="ada",
    )(c, w_ada, b_ada.reshape(n_layers, 1, n))


def _normmod_kernel(x_ref, w_ref, sh_ref, sc_ref, o_ref):
    y = _rms(x_ref[...], w_ref[...])
    o_ref[...] = (y * (1.0 + sc_ref[...]) + sh_ref[...]).astype(o_ref.dtype)


def _norm_kernel(x_ref, w_ref, o_ref):
    o_ref[...] = _rms(x_ref[...], w_ref[...]).astype(o_ref.dtype)


def _row_blocks(b, s):
    if s >= 512:
        return 1, _pick(s, 512, 16)
    return _pick(b, max(1, 512 // s), 1), s


def _norm_vmem_mb(block_elems, out_dtype):
    return block_elems * (2 * 4 + 2 * jnp.dtype(out_dtype).itemsize + 2 * 4) // 2 ** 20 + 4


def _normmod(x, w, layer, mod, k_shift, k_scale, out_dtype):
    b, s, d = x.shape
    nb, ts = _row_blocks(b, s)
    w3 = w.reshape(w.shape[0], 1, d)
    return pl.pallas_call(
        _normmod_kernel,
        grid=(b // nb, s // ts),
        in_specs=[pl.BlockSpec((nb, ts, d), lambda i, j: (i, j, 0)),
                  pl.BlockSpec((1, 1, d), lambda i, j: (layer, 0, 0)),
                  pl.BlockSpec((nb, None, 1, d), lambda i, j: (i, k_shift, 0, 0)),
                  pl.BlockSpec((nb, None, 1, d), lambda i, j: (i, k_scale, 0, 0))],
        out_specs=pl.BlockSpec((nb, ts, d), lambda i, j: (i, j, 0)),
        out_shape=jax.ShapeDtypeStruct((b, s, d), out_dtype),
        compiler_params=_cparams(("parallel", "parallel"), _norm_vmem_mb(nb * ts * d, out_dtype)),
        name="normmod",
    )(x, w3, mod, mod)


def _final_norm(x, w):
    b, s, d = x.shape
    nb, ts = _row_blocks(b, s)
    return pl.pallas_call(
        _norm_kernel,
        grid=(b // nb, s // ts),
        in_specs=[pl.BlockSpec((nb, ts, d), lambda i, j: (i, j, 0)),
                  pl.BlockSpec((1, 1, d), lambda i, j: (0, 0, 0))],
        out_specs=pl.BlockSpec((nb, ts, d), lambda i, j: (i, j, 0)),
        out_shape=jax.ShapeDtypeStruct((b, s, d), F32),
        compiler_params=_cparams(("parallel", "parallel"), _norm_vmem_mb(nb * ts * d, F32)),
        name="final_norm",
    )(x, w.reshape(1, 1, d))


def _mm(name, grid, a, a_spec, b, b_spec, extras, outs, epilogue, *, trans_b=False, sem, vmem_mb):
    n_ex, n_out = len(extras), len(outs)
    dn = (((1,), (1 if trans_b else 0,)), ((), ()))

    def kern(*refs):
        a_blk = refs[0][...]
        a_blk = a_blk.reshape(-1, a_blk.shape[-1]).astype(BF16)
        acc = lax.dot_general(a_blk, refs[1][...].astype(BF16), dn, preferred_element_type=F32)
        epilogue(acc, refs[2:2 + n_ex], refs[2 + n_ex:2 + n_ex + n_out])

    return pl.pallas_call(
        kern,
        grid=grid,
        in_specs=[a_spec, b_spec] + [s for _, s in extras],
        out_specs=[s for _, s in outs],
        out_shape=[sd for sd, _ in outs],
        compiler_params=_cparams(sem, vmem_mb),
        name=name,
    )(a, b, *[x for x, _ in extras])


def _rope_tables(pos):
    half = 32
    inv = jnp.power(ROPE_THETA, -jnp.arange(half, dtype=F32) / half)
    ang = pos.astype(F32)[:, None] * inv[None, :]
    return jnp.tile(jnp.cos(ang), (1, 4)), jnp.tile(jnp.sin(ang), (1, 4))


def _mla_project(u2, pos_rows, w_in, q_norm_w, kv_norm_w, layer, q_lora, kv_lora):
    m, d = u2.shape
    n = w_in.shape[1]
    period = pos_rows.shape[0]
    tm = _pick(period, 512, 16)
    cos, sin = _rope_tables(pos_rows)
    npb = period // tm

    def epilogue(acc, ex, o):
        qw_ref, kw_ref, cos_ref, sin_ref = ex
        o[0][...] = _rms(acc[:, :q_lora], qw_ref[...]).astype(BF16)
        ckv = _rms(acc[:, q_lora:q_lora + kv_lora], kw_ref[...])
        o[1][...] = ckv
        o[2][...] = ckv.astype(BF16)
        kpe = _rope(acc[:, q_lora + kv_lora:], cos_ref, sin_ref)[:, :64]
        o[3][...] = kpe
        o[4][...] = kpe.astype(BF16)

    row = lambda width: pl.BlockSpec((tm, width), lambda i: (i, 0))
    return _mm(
        "mla_in", (m // tm,), u2, pl.BlockSpec((tm, d), lambda i: (i, 0)),
        w_in, pl.BlockSpec((d, n), lambda i: (0, 0)),
        [(q_norm_w.reshape(-1, 1, q_lora), pl.BlockSpec((None, 1, q_lora), lambda i: (layer, 0, 0))),
         (kv_norm_w.reshape(-1, 1, kv_lora), pl.BlockSpec((None, 1, kv_lora), lambda i: (layer, 0, 0))),
         (cos, pl.BlockSpec((tm, 128), lambda i: (i % npb, 0))),
         (sin, pl.BlockSpec((tm, 128), lambda i: (i % npb, 0)))],
        [(jax.ShapeDtypeStruct((m, q_lora), BF16), row(q_lora)),
         (jax.ShapeDtypeStruct((m, kv_lora), F32), row(kv_lora)),
         (jax.ShapeDtypeStruct((m, kv_lora), BF16), row(kv_lora)),
         (jax.ShapeDtypeStruct((m, 64), F32), row(64)),
         (jax.ShapeDtypeStruct((m, 64), BF16), row(64))],
        epilogue, sem=("parallel",), vmem_mb=(4 * d * n + 4 * tm * d + 16 * tm * n) // 2 ** 20 + 8)


def _q_heads(cq, w_nope, w_pe, pos_rows, scale):
    m, kq = cq.shape
    period = pos_rows.shape[0]
    tm = _pick(period, 512, 16)
    npb = period // tm
    cos, sin = _rope_tables(pos_rows)

    def plain(acc, ex, o):
        o[0][...] = (acc * scale).astype(BF16)

    def roped(acc, ex, o):
        o[0][...] = (_rope(acc, ex[0], ex[1]) * scale).astype(BF16)

    outs = []
    for name, w, extras, epi in (("q_nope", w_nope, [], plain),
                                 ("q_pe", w_pe, [(cos, pl.BlockSpec((tm, 128), lambda i, j: (i % npb, 0))),
                                                 (sin, pl.BlockSpec((tm, 128), lambda i, j: (i % npb, 0)))], roped)):
        n = w.shape[1]
        tn = _pick(n, 2048, 128)
        outs.append(_mm(name, (m // tm, n // tn), cq, pl.BlockSpec((tm, kq), lambda i, j: (i, 0)),
                        w, pl.BlockSpec((kq, tn), lambda i, j: (0, j)), extras,
                        [(jax.ShapeDtypeStruct((m, n), BF16), pl.BlockSpec((tm, tn), lambda i, j: (i, j)))],
                        epi, sem=("parallel", "parallel"), vmem_mb=40)[0])
    return outs


def _kv_heads(ckv_b, w_ukv3, layer, n_heads):
    m, kc = ckv_b.shape
    tm = _pick(m, 512, 16)
    hg = 4 if n_heads % 4 == 0 else 1

    def kern(c_ref, w_ref, k_ref, v_ref):
        c = c_ref[...]
        for g in range(n_heads // hg):
            acc = jnp.dot(c, w_ref[:, g * hg * 256:(g + 1) * hg * 256].astype(BF16), preferred_element_type=F32)
            for h in range(hg):
                cols = slice((g * hg + h) * 128, (g * hg + h + 1) * 128)
                k_ref[:, cols] = acc[:, h * 256:h * 256 + 128].astype(BF16)
                v_ref[:, cols] = acc[:, h * 256 + 128:(h + 1) * 256].astype(BF16)

    out = jax.ShapeDtypeStruct((m, n_heads * 128), BF16)
    out_spec = pl.BlockSpec((tm, n_heads * 128), lambda i: (i, 0))
    return pl.pallas_call(
        kern,
        grid=(m // tm,),
        in_specs=[pl.BlockSpec((tm, kc), lambda i: (i, 0)),
                  pl.BlockSpec((None, kc, n_heads * 256), lambda i: (layer, 0, 0), pipeline_mode=pl.Buffered(1))],
        out_specs=[out_spec, out_spec],
        out_shape=[out, out],
        compiler_params=_cparams(("parallel",), (4 * kc * n_heads * 256 + 8 * tm * n_heads * 128) // 2 ** 20 + 16),
        name="kv_heads",
    )(ckv_b, w_ukv3)


def _gate_specs(x, mod, k_gate, tm, tn):
    b, s, d = x.shape
    x2 = x.reshape(b * s, d)
    x_spec = pl.BlockSpec((tm, tn), lambda i, j: (i, j))
    if s % tm == 0:
        per = s // tm
        g = (mod, pl.BlockSpec((None, None, 1, tn), lambda i, j: (i // per, k_gate, 0, j)))
    else:
        rows = jnp.repeat(mod[:, k_gate, 0, :], s, axis=0)
        g = (rows, pl.BlockSpec((tm, tn), lambda i, j: (i, j)))
    return [(x2, x_spec), g]


LHS_TILE_BYTES = 12 * 2 ** 20


def _proj_residual(name, a2, w, layer, x, mod, k_gate):
    m, kk = a2.shape
    n = w.shape[-1]
    b, s, _ = x.shape
    rows = min(1024, LHS_TILE_BYTES // (2 * kk))
    tm = _pick(s, rows, 16) if s >= 256 else _pick(m, rows, 16)
    tn = _pick(n, 512, 128)
    w_spec = (pl.BlockSpec((None, kk, tn), lambda i, j: (layer, 0, j)) if w.ndim == 3
              else pl.BlockSpec((kk, tn), lambda i, j: (0, j)))

    def epilogue(acc, ex, o):
        o[0][...] = ex[0][...] + ex[1][...] * acc

    out = _mm(name, (m // tm, n // tn), a2, pl.BlockSpec((tm, kk), lambda i, j: (i, 0)),
              w, w_spec, _gate_specs(x, mod, k_gate, tm, tn),
              [(jax.ShapeDtypeStruct((m, n), F32), pl.BlockSpec((tm, tn), lambda i, j: (i, j)))],
              epilogue, sem=("parallel", "arbitrary"),
              vmem_mb=(4 * tm * kk + (10 if w.ndim == 3 else 4) * kk * tn + 36 * tm * tn) // 2 ** 20 + 8)[0]
    return out.reshape(b, s, n)


def _flash_kernel(qn_ref, qp_ref, kn_ref, v_ref, kpe_ref, wg_ref, wv_ref, o_ref, wgb_ref, wvb_ref, *, t, nq, n_cast):
    @pl.when(pl.program_id(0) * pl.num_programs(1) + pl.program_id(1) < n_cast)
    def _():
        wgb_ref[...] = wg_ref[...].astype(BF16)
        wvb_ref[...] = wv_ref[...].astype(BF16)

    nt = (((1,), (1,)), ((), ()))
    kpe = kpe_ref[0]
    qc = lax.broadcasted_iota(jnp.int32, (t, t), 0) // CHUNK
    kc = lax.broadcasted_iota(jnp.int32, (t, t), 1) // CHUNK
    diag_visible = kc <= qc
    for hh in range(2):
        hs = slice(hh * 128, (hh + 1) * 128)
        kcat = jnp.concatenate([kn_ref[0, :, hs], kpe], axis=-1)
        for qi in range(nq):
            lo = qi * t
            rows = slice(lo, lo + t)
            q = jnp.concatenate([qn_ref[0, rows, hs], qp_ref[0, rows, hh * 64:(hh + 1) * 64]], axis=-1)
            s_d = jnp.where(diag_visible, lax.dot_general(q, kcat[lo:lo + t], nt, preferred_element_type=F32), NEG_INF)
            m = jnp.max(s_d, axis=-1, keepdims=True)
            if qi > 0:
                s_o = lax.dot_general(q, kcat[:lo], nt, preferred_element_type=F32)
                m = jnp.maximum(m, jnp.max(s_o, axis=-1, keepdims=True))
                p_o = jnp.exp(s_o - m)
            p_d = jnp.exp(s_d - m)
            l = jnp.sum(p_d, axis=-1, keepdims=True)
            acc = jnp.dot(p_d.astype(BF16), v_ref[0, rows, hs], preferred_element_type=F32)
            if qi > 0:
                l = l + jnp.sum(p_o, axis=-1, keepdims=True)
                acc = acc + jnp.dot(p_o.astype(BF16), v_ref[0, 0:lo, hs], preferred_element_type=F32)
            o_ref[0, rows, hs] = (acc / l).astype(BF16)


def _flash(qn, qp, kn, v, kpe, n_heads, w_up, layer):
    b, s, _ = qn.shape
    d, f = w_up.shape[1], w_up.shape[2] // 2
    t = _pick(s, 256, CHUNK)
    tf = _pick(f, 256, 128)
    n_cast = f // tf
    hp_steps = n_heads // 2
    assert n_cast <= b * hp_steps
    cast_tile = lambda bb, hp: jnp.minimum(bb * hp_steps + hp, n_cast - 1)
    kern = functools.partial(_flash_kernel, t=t, nq=s // t, n_cast=n_cast)
    wide = pl.BlockSpec((1, s, 256), lambda bb, hp: (bb, 0, hp))
    wb_spec = pl.BlockSpec((d, tf), lambda bb, hp: (0, cast_tile(bb, hp)))
    return pl.pallas_call(
        kern,
        grid=(b, hp_steps),
        in_specs=[wide, pl.BlockSpec((1, s, 128), lambda bb, hp: (bb, 0, hp)), wide, wide,
                  pl.BlockSpec((1, s, 64), lambda bb, hp: (bb, 0, 0)),
                  pl.BlockSpec((None, d, tf), lambda bb, hp: (layer, 0, cast_tile(bb, hp))),
                  pl.BlockSpec((None, d, tf), lambda bb, hp: (layer, 0, n_cast + cast_tile(bb, hp)))],
        out_specs=[wide, wb_spec, wb_spec],
        out_shape=[jax.ShapeDtypeStruct((b, s, n_heads * 128), BF16),
                   jax.ShapeDtypeStruct((d, f), BF16), jax.ShapeDtypeStruct((d, f), BF16)],
        compiler_params=_cparams(("arbitrary", "arbitrary"), 30 + 24 * d * tf // 2 ** 20),
        name="flash_prompt",
    )(qn, qp, kn, v, kpe, w_up, w_up)


def _sattn_kernel(ql_ref, qp_ref, ck_ref, kp_ref, nck_ref, nkp_ref, o_ref, *, past, n_split):
    nt = (((1,), (1,)), ((), ()))
    n_heads, ds, c = ql_ref.shape[1:]
    ck = ck_ref[0].astype(BF16)
    kp = kp_ref[0].astype(BF16)
    nck = nck_ref[0].astype(BF16)
    nkp = nkp_ref[0].astype(BF16)
    hg = n_heads // n_split
    rows = hg * ds
    qpos = past + lax.broadcasted_iota(jnp.int32, (rows, ds), 0) % ds
    kpos = past + lax.broadcasted_iota(jnp.int32, (rows, ds), 1)
    new_visible = kpos // CHUNK <= qpos // CHUNK
    for g in range(n_split):
        ql = ql_ref[0, g * hg:(g + 1) * hg].reshape(rows, c)
        qp = qp_ref[0, g * hg:(g + 1) * hg].reshape(rows, qp_ref.shape[-1])
        s_c = lax.dot_general(ql, ck, nt, preferred_element_type=F32) + jnp.dot(qp, kp, preferred_element_type=F32)
        s_n = lax.dot_general(ql, nck, nt, preferred_element_type=F32) + lax.dot_general(qp, nkp, nt, preferred_element_type=F32)
        s_n = jnp.where(new_visible, s_n, NEG_INF)
        m = jnp.maximum(jnp.max(s_c, axis=-1, keepdims=True), jnp.max(s_n, axis=-1, keepdims=True))
        p_c = jnp.exp(s_c - m)
        p_n = jnp.exp(s_n - m)
        l = jnp.sum(p_c, axis=-1, keepdims=True) + jnp.sum(p_n, axis=-1, keepdims=True)
        acc = (jnp.dot(p_c.astype(BF16), ck, preferred_element_type=F32)
               + jnp.dot(p_n.astype(BF16), nck, preferred_element_type=F32))
        o_ref[0, g * hg:(g + 1) * hg] = (acc / l).astype(BF16).reshape(hg, ds, c)


def _sample_attention(q_lat, q_pe, cache_ckv, cache_kpe_t, layer, ckv_new, kpe_new):
    b, n_heads, ds, c = q_lat.shape
    past = cache_ckv.shape[2]
    n_split = 2 if n_heads % 2 == 0 else 1
    kern = functools.partial(_sattn_kernel, past=past, n_split=n_split)
    per_stream = lambda *blk: pl.BlockSpec((1,) + blk, lambda bb: (bb,) + (0,) * len(blk))
    rows = n_heads * ds // n_split
    return pl.pallas_call(
        kern,
        grid=(b,),
        in_specs=[per_stream(n_heads, ds, c), per_stream(n_heads, ds, 64),
                  pl.BlockSpec((None, 1, past, c), lambda bb: (layer, bb, 0, 0)),
                  pl.BlockSpec((None, 1, 64, past), lambda bb: (layer, bb, 0, 0)),
                  per_stream(ds, c), per_stream(ds, 64)],
        out_specs=per_stream(n_heads, ds, c),
        out_shape=jax.ShapeDtypeStruct((b, n_heads, ds, c), BF16),
        compiler_params=_cparams(("parallel",), (2 * 4 * past * (c + 128) + 2 * past * c + 12 * rows * past) // 2 ** 20 + 10),
        name="attn_sample",
    )(q_lat, q_pe, cache_ckv, cache_kpe_t, ckv_new, kpe_new)


def _absorb_q(qn, w_ukv3, layer, n_heads, b, ds):
    m = qn.shape[0]
    c = w_ukv3.shape[1]

    def epilogue(acc, ex, o):
        o[0][...] = acc.astype(BF16).reshape(b, ds, c)

    return _mm("absorb_q", (n_heads,), qn, pl.BlockSpec((m, 128), lambda h: (0, h)),
               w_ukv3, pl.BlockSpec((None, c, 128), lambda h: (layer, 0, 2 * h)), [],
               [(jax.ShapeDtypeStruct((b, n_heads, ds, c), BF16), pl.BlockSpec((b, None, ds, c), lambda h: (0, h, 0, 0)))],
               epilogue, trans_b=True, sem=("parallel",), vmem_mb=16)[0]


def _expand_v(o_lat, w_ukv3, layer):
    b, n_heads, ds, c = o_lat.shape
    m = b * ds

    def epilogue(acc, ex, o):
        o[0][...] = acc.astype(BF16)

    return _mm("expand_v", (n_heads,), o_lat, pl.BlockSpec((b, None, ds, c), lambda h: (0, h, 0, 0)),
               w_ukv3, pl.BlockSpec((None, c, 128), lambda h: (layer, 0, 2 * h + 1)), [],
               [(jax.ShapeDtypeStruct((m, n_heads * 128), BF16), pl.BlockSpec((m, 128), lambda h: (0, h)))],
               epilogue, sem=("parallel",), vmem_mb=16)[0]


def _pool_kernel(u_ref, prev_ref, w_ref, ps_ref, x_ref, g_ref, o_ref, st, *, ts, pos0, prev_rows, seq_tiles):
    grp = pl.program_id(0)
    sidx = pl.program_id(2)
    nb, gw = u_ref.shape[0], u_ref.shape[2]
    total = POOL_HALO + ts
    u = u_ref[...]
    if prev_rows == POOL_HALO:
        halo = prev_ref[...]
        st[:, 0:POOL_HALO, :] = jnp.where(sidx % seq_tiles == 0, jnp.zeros_like(halo), halo)
    else:
        st[:, 0:POOL_HALO, :] = jnp.zeros((nb, POOL_HALO, gw), F32)
        st[:, POOL_HALO - prev_rows:POOL_HALO, :] = prev_ref[...]
    st[:, POOL_HALO:total, :] = u
    s = st[...].reshape(nb * total, gw)
    sums = []
    for k in range(len(POOL_WINDOWS)):
        s = s + pltpu.roll(s, 2 ** k, 0)
        sums.append(s)
    win = sums[-1]
    for k in range(len(POOL_WINDOWS) - 2, -1, -1):
        win = jnp.where(grp == k, sums[k], win)
    win = win.reshape(nb, total, gw)[:, POOL_HALO:, :]
    wsize = jnp.left_shift(2, grp)
    pos = pos0 + (sidx % seq_tiles) * ts + lax.broadcasted_iota(jnp.int32, (nb, ts, gw), 1)
    cnt = jnp.minimum(pos + 1, wsize).astype(F32)
    d = (win / cnt - u).astype(BF16).reshape(nb * ts, gw)
    y = jnp.dot(d, w_ref[...].astype(BF16), preferred_element_type=F32).reshape(nb, ts, gw)
    o_ref[...] = x_ref[...] + g_ref[...] * (y * ps_ref[...])


def _pool_mix(u, prev, w_pool, pool_scale, layer_j, x, mod, k_gate, pos0):
    b, s, d = u.shape
    ng = w_pool.shape[1]
    gw = d // ng
    nb, ts = _row_blocks(b, s)
    if prev is None:
        nb, ts = 1, _pick(s, 512, POOL_HALO)
        seq_tiles = s // ts
        per = ts // POOL_HALO
        prev_arr, prev_rows = u, POOL_HALO
        prev_spec = pl.BlockSpec((1, POOL_HALO, gw), lambda g, i, j: (i, jnp.maximum(j * per - 1, 0), g))
    else:
        nb = _pick(b, 8, 1)
        seq_tiles = 1
        prev_rows = prev.shape[2]
        prev_arr = prev
        prev_spec = pl.BlockSpec((None, nb, prev_rows, gw), lambda g, i, j: (layer_j, i, 0, g))
    kern = functools.partial(_pool_kernel, ts=ts, pos0=pos0, prev_rows=prev_rows, seq_tiles=seq_tiles)
    blk = pl.BlockSpec((nb, ts, gw), lambda g, i, j: (i, j, g))
    return pl.pallas_call(
        kern,
        grid=(ng, b // nb, s // ts),
        in_specs=[blk, prev_spec,
                  pl.BlockSpec((None, None, gw, gw), lambda g, i, j: (layer_j, g, 0, 0)),
                  pl.BlockSpec((None, 1, 1, gw), lambda g, i, j: (layer_j, 0, 0, g)),
                  blk,
                  pl.BlockSpec((nb, None, 1, gw), lambda g, i, j: (i, k_gate, 0, g))],
        out_specs=blk,
        out_shape=jax.ShapeDtypeStruct((b, s, d), F32),
        scratch_shapes=[pltpu.VMEM((nb, POOL_HALO + ts, gw), F32)],
        compiler_params=_cparams(("arbitrary", "arbitrary", "arbitrary"), 48),
        name="pool_mix",
    )(u, prev_arr, w_pool, pool_scale.reshape(pool_scale.shape[0], 1, 1, d), x, mod)


def _conv_gate(hs_g, hs_v, base, rows, cw_g, cw_v, cb_g, cb_v, seq_axis):
    def conv(hs, cw, cb):
        def sl(off):
            idx = [slice(None)] * len(hs.shape)
            idx[seq_axis] = slice(base - off, base - off + rows)
            return hs[tuple(idx)]
        return cb[...] + sl(2) * cw[0:1, :] + sl(1) * cw[1:2, :] + sl(0) * cw[2:3, :]
    return _silu(conv(hs_g, cw_g, cb_g)) * conv(hs_v, cw_v, cb_v)


def _up_prompt_kernel(u_ref, wg_ref, wv_ref, cwg_ref, cwv_ref, cbg_ref, cbv_ref, wd_ref,
                      act_ref, sg_ref, sv_ref, wdb_ref, *, tm, n_sub):
    wdb_ref[...] = wd_ref[...].astype(BF16)
    tf = wg_ref.shape[1]
    streams = ((wg_ref, cwg_ref, cbg_ref), (wv_ref, cwv_ref, cbv_ref))
    sub = lax.broadcasted_iota(jnp.int32, (8, tf), 0)
    tails = [[jnp.zeros((8, tf), F32)] * 2 for _ in streams]
    for r in range(n_sub):
        a = u_ref[0, r * tm:(r + 1) * tm, :]
        conv = []
        for si, (w_ref, cw, cb) in enumerate(streams):
            h = jnp.dot(a, w_ref[...], preferred_element_type=F32)
            acc = cb[...] + h * cw[2:3, :]
            for k in (1, 2):
                rk = pltpu.roll(h, k, 0)
                hk = jnp.concatenate([jnp.where(sub < k, tails[si][k - 1], rk[:8]), rk[8:]], axis=0)
                tails[si][k - 1] = rk[:8]
                acc = acc + hk * cw[2 - k:3 - k, :]
            conv.append(acc)
        act_ref[0, r * tm:(r + 1) * tm, :] = (_silu(conv[0]) * conv[1]).astype(BF16)
    sg_ref[0] = tails[0][1][0:2]
    sv_ref[0] = tails[1][1][0:2]


def _up_sample_kernel(u_ref, pg_ref, pv_ref, wg_ref, wv_ref, cwg_ref, cwv_ref, cbg_ref, cbv_ref,
                      act_ref, sg_ref, sv_ref, *rest, nb, ds, emit):
    hs_g, hs_v = rest[-2:]
    a = u_ref[...]
    for si, (w_ref, p_ref, hs, s_ref) in enumerate(((wg_ref, pg_ref, hs_g, sg_ref), (wv_ref, pv_ref, hs_v, sv_ref))):
        w = w_ref[...]
        if emit:
            w = w.astype(BF16)
            rest[si][...] = w
        h = jnp.dot(a, w, preferred_element_type=F32)
        hs[:, 6:8, :] = p_ref[...]
        hs[:, 8:8 + ds, :] = h.reshape(nb, ds, h.shape[-1])
        s_ref[...] = hs[:, 6 + ds:8 + ds, :]
    act = _conv_gate(hs_g, hs_v, 8, ds, cwg_ref, cwv_ref, cbg_ref, cbv_ref, 1)
    act_ref[...] = act.reshape(nb * ds, act.shape[-1]).astype(BF16)


def _ffn_up_sample(h, w_gate, w_value, conv_w, conv_b, layer, state):
    b, s, d = h.shape
    emit = w_gate.ndim == 3
    f = w_gate.shape[-1] // 2 if emit else w_gate.shape[-1]
    tf = _pick(f, 256, 128)
    nf = f // tf
    cb3 = conv_b.reshape(conv_b.shape[0], 1, 2 * f)
    g2 = lambda j: (layer, 0, j)
    v2 = lambda j: (layer, 0, nf + j)
    col = lambda j: (0, j)
    st_spec = pl.BlockSpec((b, 2, tf), lambda j: (0, 0, j))
    w_specs = ([pl.BlockSpec((None, d, tf), g2), pl.BlockSpec((None, d, tf), v2)] if emit
               else [pl.BlockSpec((d, tf), col), pl.BlockSpec((d, tf), col)])
    copies = [jax.ShapeDtypeStruct((d, f), BF16)] * 2 if emit else []
    outs = pl.pallas_call(
        functools.partial(_up_sample_kernel, nb=b, ds=s, emit=emit),
        grid=(nf,),
        in_specs=[pl.BlockSpec((b * s, d), lambda j: (0, 0)),
                  pl.BlockSpec((None, b, 2, tf), lambda j: (layer, 0, 0, j)),
                  pl.BlockSpec((None, b, 2, tf), lambda j: (layer, 0, 0, nf + j))]
                 + w_specs
                 + [pl.BlockSpec((None, 3, tf), g2), pl.BlockSpec((None, 3, tf), v2),
                    pl.BlockSpec((None, 1, tf), g2), pl.BlockSpec((None, 1, tf), v2)],
        out_specs=[pl.BlockSpec((b * s, tf), col), st_spec, st_spec] + [pl.BlockSpec((d, tf), col)] * len(copies),
        out_shape=[jax.ShapeDtypeStruct((b * s, f), BF16),
                   jax.ShapeDtypeStruct((b, 2, f), F32), jax.ShapeDtypeStruct((b, 2, f), F32)] + copies,
        scratch_shapes=[pltpu.VMEM((b, 8 + s, tf), F32), pltpu.VMEM((b, 8 + s, tf), F32)],
        compiler_params=_cparams(("arbitrary",), 40),
        name="ffn_up_sample",
    )(h.reshape(b * s, d), state, state, w_gate, w_value, conv_w, conv_w, cb3, cb3)
    return (outs[0], jnp.concatenate(outs[1:3], axis=-1)) + tuple(outs[3:])


def _ffn_up_prompt(h, wgb, wvb, conv_w, conv_b, w_down, layer):
    b, s, d = h.shape
    f = wgb.shape[1]
    tf = min(f, 512)
    n_col = pl.cdiv(f, tf)
    n_sub = 8 if s % 128 == 0 else 1
    tr = -(-pl.cdiv(f, n_col) // 16) * 16
    tc = d // b
    assert tc % 128 == 0 and (n_col - 1) * tr < f <= n_col * tr
    col = lambda bb, j: (0, j)
    st_spec = pl.BlockSpec((1, 2, tf), lambda bb, j: (bb, 0, j))
    cw, cb = conv_w[layer], conv_b[layer].reshape(1, 2 * f)
    act, sg, sv, w_down_b = pl.pallas_call(
        functools.partial(_up_prompt_kernel, tm=s // n_sub, n_sub=n_sub),
        grid=(b, n_col),
        in_specs=[pl.BlockSpec((1, s, d), lambda bb, j: (bb, 0, 0), pipeline_mode=pl.Buffered(1)),
                  pl.BlockSpec((d, tf), col), pl.BlockSpec((d, tf), col),
                  pl.BlockSpec((3, tf), col), pl.BlockSpec((3, tf), col),
                  pl.BlockSpec((1, tf), col), pl.BlockSpec((1, tf), col),
                  pl.BlockSpec((None, tr, tc), lambda bb, j: (layer, j, bb))],
        out_specs=[pl.BlockSpec((1, s, tf), lambda bb, j: (bb, 0, j)), st_spec, st_spec,
                   pl.BlockSpec((tr, tc), lambda bb, j: (j, bb))],
        out_shape=[jax.ShapeDtypeStruct((b, s, f), BF16),
                   jax.ShapeDtypeStruct((b, 2, f), F32), jax.ShapeDtypeStruct((b, 2, f), F32),
                   jax.ShapeDtypeStruct((f, d), BF16)],
        compiler_params=_cparams(("arbitrary", "arbitrary"),
                                 (2 * s * d + 8 * d * tf + 40 * s * tf + 12 * tr * tc) // 2 ** 20 + 8),
        name="ffn_up_prompt",
    )(h, wgb, wvb, cw[:, :f], cw[:, f:], cb[:, :f], cb[:, f:], w_down)
    return act.reshape(b * s, f), jnp.concatenate([sg, sv], axis=-1), w_down_b


def kernel(x_prompt, x_sample, c_prompt, c_sample, cache_ckv, cache_kpe, state_pool, state_conv, w_ada, b_ada,
           norm_mix_w, norm_ffn_w, w_dq, q_norm_w, w_uq, w_dkv, kv_norm_w, w_ukv, w_o, w_pool, pool_scale,
           w_up, conv_w, conv_b, w_down, final_norm_w):
    bp, s, d = x_prompt.shape
    bs, ds, _ = x_sample.shape
    past = cache_ckv.shape[2]
    depth = w_ada.shape[0]
    n_heads = w_uq.shape[2]
    q_lora = w_dq.shape[2]
    kv_lora = w_dkv.shape[2] - 64
    qk_head = w_uq.shape[3]
    scale = float(qk_head) ** -0.5
    pos_p = jnp.arange(s, dtype=jnp.int32)
    pos_s = jnp.tile(past + jnp.arange(ds, dtype=jnp.int32), bs)

    nb_all = bp + bs
    pad = (-nb_all) % 16
    c_all = jnp.concatenate([c_prompt, c_sample, jnp.zeros((pad, d), F32)], axis=0)
    mod_all = _ada(c_all, w_ada, b_ada)

    xp, xs = x_prompt, x_sample
    ckv_p_l, kpe_p_l, ckv_s_l, kpe_s_l = [], [], [], []
    pool_p_l, pool_s_l, conv_p_l, conv_s_l = [], [], [], []
    for i in range(depth):
        j = i // 2
        mod_p = mod_all[i, :bp].reshape(bp, 6, 1, d)
        mod_s = mod_all[i, bp:nb_all].reshape(bs, 6, 1, d)
        w_up_b = None
        if i % 2 == 0:
            w_in = jnp.concatenate([w_dq[j], w_dkv[j], jnp.zeros((d, 64), F32)], axis=1).astype(BF16)
            w_q_nope = w_uq[j][:, :, :128].reshape(q_lora, n_heads * 128).astype(BF16)
            w_q_pe = w_uq[j][:, :, 128:].reshape(q_lora, n_heads * 64).astype(BF16)
            w_ukv3 = w_ukv.reshape(w_ukv.shape[0], kv_lora, n_heads * 256)
            up = _normmod(xp, norm_mix_w, i, mod_p, 0, 1, BF16)
            cq, ckv_p, ckv_pb, kpe_p, kpe_pb = _mla_project(up.reshape(bp * s, d), pos_p, w_in, q_norm_w, kv_norm_w, j,
                                                            q_lora, kv_lora)
            qn, qp = _q_heads(cq, w_q_nope, w_q_pe, pos_p, scale)
            kn, vv = _kv_heads(ckv_pb, w_ukv3, j, n_heads)
            r3 = lambda t: t.reshape(bp, s, t.shape[-1])
            o_p, *w_up_b = _flash(r3(qn), r3(qp), r3(kn), r3(vv), r3(kpe_pb), n_heads, w_up, i)
            xp = _proj_residual("attn_out_prompt", o_p.reshape(bp * s, n_heads * 128), w_o, j, xp, mod_p, 2)
            us = _normmod(xs, norm_mix_w, i, mod_s, 0, 1, BF16)
            cq, ckv_s, _, kpe_s, _ = _mla_project(us.reshape(bs * ds, d), pos_s, w_in, q_norm_w, kv_norm_w, j,
                                                  q_lora, kv_lora)
            qn, qp = _q_heads(cq, w_q_nope, w_q_pe, pos_s, scale)
            q_lat = _absorb_q(qn, w_ukv3, j, n_heads, bs, ds)
            qp_heads = qp.reshape(bs, ds, n_heads, 64).transpose(0, 2, 1, 3)
            o_lat = _sample_attention(q_lat, qp_heads, cache_ckv, jnp.swapaxes(cache_kpe, 2, 3), j, ckv_s.reshape(bs, ds, kv_lora),
                                      kpe_s.reshape(bs, ds, 64))
            o_s = _expand_v(o_lat, w_ukv3, j)
            xs = _proj_residual("attn_out_sample", o_s, w_o, j, xs, mod_s, 2)
            ckv_p_l.append(ckv_p.reshape(bp, s, kv_lora))
            kpe_p_l.append(kpe_p.reshape(bp, s, 64))
            ckv_s_l.append(ckv_s.reshape(bs, ds, kv_lora))
            kpe_s_l.append(kpe_s.reshape(bs, ds, 64))
        else:
            hist = state_pool.shape[2]
            up = _normmod(xp, norm_mix_w, i, mod_p, 0, 1, F32)
            us = _normmod(xs, norm_mix_w, i, mod_s, 0, 1, F32)
            pool_p_l.append(up[:, s - hist:])
            pool_s_l.append(jnp.concatenate([state_pool[j], us], axis=1)[:, -hist:])
            xp = _pool_mix(up, None, w_pool, pool_scale, j, xp, mod_p, 2, 0)
            xs = _pool_mix(us, state_pool, w_pool, pool_scale, j, xs, mod_s, 2, past)
        hp = _normmod(xp, norm_ffn_w, i, mod_p, 3, 4, BF16)
        hs = _normmod(xs, norm_ffn_w, i, mod_s, 3, 4, BF16)
        if w_up_b is None:
            act_s, cst_s, *w_up_b = _ffn_up_sample(hs, w_up, w_up, conv_w, conv_b, i, state_conv)
        else:
            act_s, cst_s = _ffn_up_sample(hs, *w_up_b, conv_w, conv_b, i, state_conv)
        act_p, cst_p, w_down_b = _ffn_up_prompt(hp, *w_up_b, conv_w, conv_b, w_down, i)
        conv_p_l.append(cst_p)
        conv_s_l.append(cst_s)
        xs = _proj_residual("ffn_down_sample", act_s, w_down_b, None, xs, mod_s, 5)
        xp = _proj_residual("ffn_down_prompt", act_p, w_down_b, None, xp, mod_p, 5)

    y_prompt = _final_norm(xp, final_norm_w)
    y_sample = _final_norm(xs, final_norm_w)
    return (y_prompt, y_sample,
            jnp.stack(ckv_p_l), jnp.stack(kpe_p_l), jnp.stack(ckv_s_l), jnp.stack(kpe_s_l),
            jnp.stack(pool_p_l), jnp.stack(pool_s_l), jnp.stack(conv_p_l), jnp.stack(conv_s_l))
```

```python
import functools

import jax
import jax.numpy as jnp
from jax import lax
from jax.experimental import pallas as pl
from jax.experimental.pallas import tpu as pltpu

F32 = jnp.float32
BF16 = jnp.bfloat16

CHUNK = 64
ROPE_THETA = 10000.0
RMS_EPS = 1e-6
NEG_INF = -1e30
POOL_WINDOWS = (2, 4, 8, 16)
POOL_HALO = 32
V7X_VMEM_CAP_MB = 56


def _cparams(sem, vmem_mb, flags=None):
    return pltpu.CompilerParams(dimension_semantics=sem, vmem_limit_bytes=min(vmem_mb, V7X_VMEM_CAP_MB) * 2 ** 20,
                                flags=flags)


def _pick(dim, pref, align):
    t = min(pref, dim)
    t -= t % align
    while t >= align:
        if dim % t == 0:
            return t
        t -= align
    return dim


def _silu(x):
    return x * (0.5 + 0.5 * jnp.tanh(0.5 * x))


def _rms(x, w):
    return x * lax.rsqrt(jnp.mean(x * x, axis=-1, keepdims=True) + RMS_EPS) * w


def _rope_rot(x):
    n = x.shape[-1]
    lane = lax.broadcasted_iota(jnp.int32, x.shape, x.ndim - 1)
    return jnp.where(lane % 64 < 32, -pltpu.roll(x, n - 32, x.ndim - 1), pltpu.roll(x, 32, x.ndim - 1))


def _rope(x, cos_ref, sin_ref):
    reps = x.shape[-1] // 128
    c = cos_ref[...]
    s = sin_ref[...]
    if reps > 1:
        c = jnp.concatenate([c] * reps, axis=-1)
        s = jnp.concatenate([s] * reps, axis=-1)
    return x * c + _rope_rot(x) * s


def _ada_kernel(c_ref, w_ref, b_ref, o_ref):
    a = _silu(c_ref[...]).astype(BF16)
    o_ref[...] = jnp.dot(a, w_ref[...].astype(BF16), preferred_element_type=F32) + b_ref[...]


def _ada(c, w_ada, b_ada):
    n_layers, d, n = w_ada.shape
    bc = c.shape[0]
    tn = _pick(n, 512, 128)
    return pl.pallas_call(
        _ada_kernel,
        grid=(n_layers, n // tn),
        in_specs=[pl.BlockSpec((bc, d), lambda l, j: (0, 0)),
                  pl.BlockSpec((None, d, tn), lambda l, j: (l, 0, j)),
                  pl.BlockSpec((None, 1, tn), lambda l, j: (l, 0, j))],
        out_specs=pl.BlockSpec((None, bc, tn), lambda l, j: (l, 0, j)),
        out_shape=jax.ShapeDtypeStruct((n_layers, bc, n), F32),
        compiler_params=_cparams(("parallel", "parallel"), 4 * d * tn * 3 // 2 ** 20 + 8),
        name="ada",
    )(c, w_ada, b_ada.reshape(n_layers, 1, n))


def _normmod_kernel(x_ref, w_ref, sh_ref, sc_ref, o_ref):
    y = _rms(x_ref[...], w_ref[...])
    o_ref[...] = (y * (1.0 + sc_ref[...]) + sh_ref[...]).astype(o_ref.dtype)


def _norm_kernel(x_ref, w_ref, o_ref):
    o_ref[...] = _rms(x_ref[...], w_ref[...]).astype(o_ref.dtype)


def _row_blocks(b, s):
    if s >= 512:
        return 1, _pick(s, 512, 16)
    return _pick(b, max(1, 512 // s), 1), s


def _norm_vmem_mb(block_elems, out_dtype):
    return block_elems * (2 * 4 + 2 * jnp.dtype(out_dtype).itemsize + 2 * 4) // 2 ** 20 + 4


def _normmod(x, w, layer, mod, k_shift, k_scale, out_dtype):
    b, s, d = x.shape
    nb, ts = _row_blocks(b, s)
    w3 = w.reshape(w.shape[0], 1, d)
    return pl.pallas_call(
        _normmod_kernel,
        grid=(b // nb, s // ts),
        in_specs=[pl.BlockSpec((nb, ts, d), lambda i, j: (i, j, 0)),
                  pl.BlockSpec((1, 1, d), lambda i, j: (layer, 0, 0)),
                  pl.BlockSpec((nb, None, 1, d), lambda i, j: (i, k_shift, 0, 0)),
                  pl.BlockSpec((nb, None, 1, d), lambda i, j: (i, k_scale, 0, 0))],
        out_specs=pl.BlockSpec((nb, ts, d), lambda i, j: (i, j, 0)),
        out_shape=jax.ShapeDtypeStruct((b, s, d), out_dtype),
        compiler_params=_cparams(("parallel", "parallel"), _norm_vmem_mb(nb * ts * d, out_dtype)),
        name="normmod",
    )(x, w3, mod, mod)


def _final_norm(x, w):
    b, s, d = x.shape
    nb, ts = _row_blocks(b, s)
    return pl.pallas_call(
        _norm_kernel,
        grid=(b // nb, s // ts),
        in_specs=[pl.BlockSpec((nb, ts, d), lambda i, j: (i, j, 0)),
                  pl.BlockSpec((1, 1, d), lambda i, j: (0, 0, 0))],
        out_specs=pl.BlockSpec((nb, ts, d), lambda i, j: (i, j, 0)),
        out_shape=jax.ShapeDtypeStruct((b, s, d), F32),
        compiler_params=_cparams(("parallel", "parallel"), _norm_vmem_mb(nb * ts * d, F32)),
        name="final_norm",
    )(x, w.reshape(1, 1, d))


def _mm(name, grid, a, a_spec, b, b_spec, extras, outs, epilogue, *, prologue=None, trans_b=False, sem, vmem_mb):
    n_ex, n_out = len(extras), len(outs)
    dn = (((1,), (1 if trans_b else 0,)), ((), ()))

    def kern(*refs):
        ex = refs[2:2 + n_ex]
        a_blk = refs[0][...]
        if prologue is not None:
            a_blk = prologue(a_blk, ex)
        a_blk = a_blk.reshape(-1, a_blk.shape[-1]).astype(BF16)
        acc = lax.dot_general(a_blk, refs[1][...].astype(BF16), dn, preferred_element_type=F32)
        epilogue(acc, ex, refs[2 + n_ex:2 + n_ex + n_out])

    return pl.pallas_call(
        kern,
        grid=grid,
        in_specs=[a_spec, b_spec] + [s for _, s in extras],
        out_specs=[s for _, s in outs],
        out_shape=[sd for sd, _ in outs],
        compiler_params=_cparams(sem, vmem_mb),
        name=name,
    )(a, b, *[x for x, _ in extras])


def _rope_tables(pos):
    half = 32
    inv = jnp.power(ROPE_THETA, -jnp.arange(half, dtype=F32) / half)
    ang = pos.astype(F32)[:, None] * inv[None, :]
    return jnp.tile(jnp.cos(ang), (1, 4)), jnp.tile(jnp.sin(ang), (1, 4))


def _mla_project(x, norm_w, norm_layer, mod, pos_rows, w_in, q_norm_w, kv_norm_w, layer, q_lora, kv_lora):
    b, s, d = x.shape
    m = b * s
    n = w_in.shape[1]
    period = pos_rows.shape[0]
    tm = _pick(period, 512, 16)
    cos, sin = _rope_tables(pos_rows)
    npb = period // tm
    if s % tm == 0:
        per = s // tm
        mods = [(mod, pl.BlockSpec((None, None, 1, d), lambda i, k=k: (i // per, k, 0, 0))) for k in (0, 1)]
    else:
        mods = [(jnp.repeat(mod[:, k, 0, :], s, axis=0), pl.BlockSpec((tm, d), lambda i: (i, 0))) for k in (0, 1)]

    def prologue(a, ex):
        return _rms(a, ex[0][...]) * (1.0 + ex[2][...]) + ex[1][...]

    def epilogue(acc, ex, o):
        qw_ref, kw_ref, cos_ref, sin_ref = ex[3:]
        o[0][...] = _rms(acc[:, :q_lora], qw_ref[...]).astype(BF16)
        ckv = _rms(acc[:, q_lora:q_lora + kv_lora], kw_ref[...])
        o[1][...] = ckv
        o[2][...] = ckv.astype(BF16)
        kpe = _rope(acc[:, q_lora + kv_lora:], cos_ref, sin_ref)[:, :64]
        o[3][...] = kpe
        o[4][...] = kpe.astype(BF16)

    row = lambda width: pl.BlockSpec((tm, width), lambda i: (i, 0))
    return _mm(
        "mla_in", (m // tm,), x.reshape(m, d), pl.BlockSpec((tm, d), lambda i: (i, 0)),
        w_in, pl.BlockSpec((d, n), lambda i: (0, 0), pipeline_mode=pl.Buffered(1)),
        [(norm_w.reshape(-1, 1, d), pl.BlockSpec((None, 1, d), lambda i: (norm_layer, 0, 0)))] + mods +
        [(q_norm_w.reshape(-1, 1, q_lora), pl.BlockSpec((None, 1, q_lora), lambda i: (layer, 0, 0))),
         (kv_norm_w.reshape(-1, 1, kv_lora), pl.BlockSpec((None, 1, kv_lora), lambda i: (layer, 0, 0))),
         (cos, pl.BlockSpec((tm, 128), lambda i: (i % npb, 0))),
         (sin, pl.BlockSpec((tm, 128), lambda i: (i % npb, 0)))],
        [(jax.ShapeDtypeStruct((m, q_lora), BF16), row(q_lora)),
         (jax.ShapeDtypeStruct((m, kv_lora), F32), row(kv_lora)),
         (jax.ShapeDtypeStruct((m, kv_lora), BF16), row(kv_lora)),
         (jax.ShapeDtypeStruct((m, 64), F32), row(64)),
         (jax.ShapeDtypeStruct((m, 64), BF16), row(64))],
        epilogue, prologue=prologue, sem=("parallel",),
        vmem_mb=(2 * d * n + (8 + 12 + (0 if s % tm == 0 else 16)) * tm * d + 16 * tm * n) // 2 ** 20 + 8)


def _q_heads(cq, w_nope, w_pe, pos_rows, scale):
    m, kq = cq.shape
    period = pos_rows.shape[0]
    tm = _pick(period, 512, 16)
    npb = period // tm
    cos, sin = _rope_tables(pos_rows)

    def plain(acc, ex, o):
        o[0][...] = (acc * scale).astype(BF16)

    def roped(acc, ex, o):
        o[0][...] = (_rope(acc, ex[0], ex[1]) * scale).astype(BF16)

    outs = []
    for name, w, extras, epi in (("q_nope", w_nope, [], plain),
                                 ("q_pe", w_pe, [(cos, pl.BlockSpec((tm, 128), lambda i, j: (i % npb, 0))),
                                                 (sin, pl.BlockSpec((tm, 128), lambda i, j: (i % npb, 0)))], roped)):
        n = w.shape[1]
        tn = _pick(n, 2048, 128)
        outs.append(_mm(name, (m // tm, n // tn), cq, pl.BlockSpec((tm, kq), lambda i, j: (i, 0)),
                        w, pl.BlockSpec((kq, tn), lambda i, j: (0, j)), extras,
                        [(jax.ShapeDtypeStruct((m, n), BF16), pl.BlockSpec((tm, tn), lambda i, j: (i, j)))],
                        epi, sem=("parallel", "parallel"), vmem_mb=40)[0])
    return outs


def _kv_heads(ckv_b, w_ukv3, layer, n_heads):
    m, kc = ckv_b.shape
    tm = _pick(m, 512, 16)
    hg = 4 if n_heads % 4 == 0 else 1

    def kern(c_ref, w_ref, k_ref, v_ref):
        c = c_ref[...]
        for g in range(n_heads // hg):
            acc = jnp.dot(c, w_ref[:, g * hg * 256:(g + 1) * hg * 256].astype(BF16), preferred_element_type=F32)
            for h in range(hg):
                cols = slice((g * hg + h) * 128, (g * hg + h + 1) * 128)
                k_ref[:, cols] = acc[:, h * 256:h * 256 + 128].astype(BF16)
                v_ref[:, cols] = acc[:, h * 256 + 128:(h + 1) * 256].astype(BF16)

    out = jax.ShapeDtypeStruct((m, n_heads * 128), BF16)
    out_spec = pl.BlockSpec((tm, n_heads * 128), lambda i: (i, 0))
    return pl.pallas_call(
        kern,
        grid=(m // tm,),
        in_specs=[pl.BlockSpec((tm, kc), lambda i: (i, 0)),
                  pl.BlockSpec((None, kc, n_heads * 256), lambda i: (layer, 0, 0), pipeline_mode=pl.Buffered(1))],
        out_specs=[out_spec, out_spec],
        out_shape=[out, out],
        compiler_params=_cparams(("parallel",), (4 * kc * n_heads * 256 + 8 * tm * n_heads * 128) // 2 ** 20 + 16),
        name="kv_heads",
    )(ckv_b, w_ukv3)


def _gate_specs(x, mod, k_gate, tm, tn):
    b, s, d = x.shape
    x2 = x.reshape(b * s, d)
    x_spec = pl.BlockSpec((tm, tn), lambda i, j: (i, j))
    if s % tm == 0:
        per = s // tm
        g = (mod, pl.BlockSpec((None, None, 1, tn), lambda i, j: (i // per, k_gate, 0, j)))
    else:
        rows = jnp.repeat(mod[:, k_gate, 0, :], s, axis=0)
        g = (rows, pl.BlockSpec((tm, tn), lambda i, j: (i, j)))
    return [(x2, x_spec), g]


LHS_TILE_BYTES = 12 * 2 ** 20


def _proj_residual(name, a2, w, layer, x, mod, k_gate):
    m, kk = a2.shape
    n = w.shape[-1]
    b, s, _ = x.shape
    rows = min(1024, LHS_TILE_BYTES // (2 * kk))
    tm = _pick(s, rows, 16) if s >= 256 else _pick(m, rows, 16)
    tn = _pick(n, 512, 128)
    w_spec = (pl.BlockSpec((None, kk, tn), lambda i, j: (layer, 0, j)) if w.ndim == 3
              else pl.BlockSpec((kk, tn), lambda i, j: (0, j)))

    def epilogue(acc, ex, o):
        o[0][...] = ex[0][...] + ex[1][...] * acc

    out = _mm(name, (m // tm, n // tn), a2, pl.BlockSpec((tm, kk), lambda i, j: (i, 0)),
              w, w_spec, _gate_specs(x, mod, k_gate, tm, tn),
              [(jax.ShapeDtypeStruct((m, n), F32), pl.BlockSpec((tm, tn), lambda i, j: (i, j)))],
              epilogue, sem=("parallel", "arbitrary"),
              vmem_mb=(4 * tm * kk + (10 if w.ndim == 3 else 4) * kk * tn + 36 * tm * tn) // 2 ** 20 + 8)[0]
    return out.reshape(b, s, n)


def _flash_kernel(qn_ref, qp_ref, kn_ref, v_ref, kpe_ref, wg_ref, wv_ref, o_ref, wgb_ref, wvb_ref, *, t, nq, n_cast):
    @pl.when(pl.program_id(0) * pl.num_programs(1) + pl.program_id(1) < n_cast)
    def _():
        wgb_ref[...] = wg_ref[...].astype(BF16)
        wvb_ref[...] = wv_ref[...].astype(BF16)

    nt = (((1,), (1,)), ((), ()))
    kpe = kpe_ref[0]
    qc = lax.broadcasted_iota(jnp.int32, (t, t), 0) // CHUNK
    kc = lax.broadcasted_iota(jnp.int32, (t, t), 1) // CHUNK
    diag_visible = kc <= qc
    for hh in range(2):
        hs = slice(hh * 128, (hh + 1) * 128)
        kcat = jnp.concatenate([kn_ref[0, :, hs], kpe], axis=-1)
        for qi in range(nq):
            lo = qi * t
            rows = slice(lo, lo + t)
            q = jnp.concatenate([qn_ref[0, rows, hs], qp_ref[0, rows, hh * 64:(hh + 1) * 64]], axis=-1)
            s_d = jnp.where(diag_visible, lax.dot_general(q, kcat[lo:lo + t], nt, preferred_element_type=F32), NEG_INF)
            m = jnp.max(s_d, axis=-1, keepdims=True)
            if qi > 0:
                s_o = lax.dot_general(q, kcat[:lo], nt, preferred_element_type=F32)
                m = jnp.maximum(m, jnp.max(s_o, axis=-1, keepdims=True))
                p_o = jnp.exp(s_o - m)
            p_d = jnp.exp(s_d - m)
            l = jnp.sum(p_d, axis=-1, keepdims=True)
            acc = jnp.dot(p_d.astype(BF16), v_ref[0, rows, hs], preferred_element_type=F32)
            if qi > 0:
                l = l + jnp.sum(p_o, axis=-1, keepdims=True)
                acc = acc + jnp.dot(p_o.astype(BF16), v_ref[0, 0:lo, hs], preferred_element_type=F32)
            o_ref[0, rows, hs] = (acc / l).astype(BF16)


def _flash(qn, qp, kn, v, kpe, n_heads, w_up, layer):
    b, s, _ = qn.shape
    d, f = w_up.shape[1], w_up.shape[2] // 2
    t = _pick(s, 512, CHUNK)
    tf = _pick(f, 256, 128)
    n_cast = f // tf
    hp_steps = n_heads // 2
    assert n_cast <= b * hp_steps
    cast_tile = lambda bb, hp: jnp.minimum(bb * hp_steps + hp, n_cast - 1)
    kern = functools.partial(_flash_kernel, t=t, nq=s // t, n_cast=n_cast)
    wide = pl.BlockSpec((1, s, 256), lambda bb, hp: (bb, 0, hp))
    wb_spec = pl.BlockSpec((d, tf), lambda bb, hp: (0, cast_tile(bb, hp)))
    return pl.pallas_call(
        kern,
        grid=(b, hp_steps),
        in_specs=[wide, pl.BlockSpec((1, s, 128), lambda bb, hp: (bb, 0, hp)), wide, wide,
                  pl.BlockSpec((1, s, 64), lambda bb, hp: (bb, 0, 0)),
                  pl.BlockSpec((None, d, tf), lambda bb, hp: (layer, 0, cast_tile(bb, hp))),
                  pl.BlockSpec((None, d, tf), lambda bb, hp: (layer, 0, n_cast + cast_tile(bb, hp)))],
        out_specs=[wide, wb_spec, wb_spec],
        out_shape=[jax.ShapeDtypeStruct((b, s, n_heads * 128), BF16),
                   jax.ShapeDtypeStruct((d, f), BF16), jax.ShapeDtypeStruct((d, f), BF16)],
        compiler_params=_cparams(("arbitrary", "arbitrary"), 30 + 24 * d * tf // 2 ** 20),
        name="flash_prompt",
    )(qn, qp, kn, v, kpe, w_up, w_up)


def _sattn_kernel(ql_ref, qp_ref, ck_ref, kp_ref, nck_ref, nkp_ref, o_ref, *, past, n_split):
    nt = (((1,), (1,)), ((), ()))
    n_heads, ds, c = ql_ref.shape[1:]
    ck = ck_ref[0].astype(BF16)
    kp = kp_ref[0].astype(BF16)
    nck = nck_ref[0].astype(BF16)
    nkp = nkp_ref[0].astype(BF16)
    hg = n_heads // n_split
    rows = hg * ds
    qpos = past + lax.broadcasted_iota(jnp.int32, (rows, ds), 0) % ds
    kpos = past + lax.broadcasted_iota(jnp.int32, (rows, ds), 1)
    new_visible = kpos // CHUNK <= qpos // CHUNK
    for g in range(n_split):
        ql = ql_ref[0, g * hg:(g + 1) * hg].reshape(rows, c)
        qp = qp_ref[0, g * hg:(g + 1) * hg].reshape(rows, qp_ref.shape[-1])
        s_c = lax.dot_general(ql, ck, nt, preferred_element_type=F32) + jnp.dot(qp, kp, preferred_element_type=F32)
        s_n = lax.dot_general(ql, nck, nt, preferred_element_type=F32) + lax.dot_general(qp, nkp, nt, preferred_element_type=F32)
        s_n = jnp.where(new_visible, s_n, NEG_INF)
        m = jnp.maximum(jnp.max(s_c, axis=-1, keepdims=True), jnp.max(s_n, axis=-1, keepdims=True))
        p_c = jnp.exp(s_c - m)
        p_n = jnp.exp(s_n - m)
        l = jnp.sum(p_c, axis=-1, keepdims=True) + jnp.sum(p_n, axis=-1, keepdims=True)
        acc = (jnp.dot(p_c.astype(BF16), ck, preferred_element_type=F32)
               + jnp.dot(p_n.astype(BF16), nck, preferred_element_type=F32))
        o_ref[0, g * hg:(g + 1) * hg] = (acc / l).astype(BF16).reshape(hg, ds, c)


def _sample_attention(q_lat, q_pe, cache_ckv, cache_kpe_t, layer, ckv_new, kpe_new):
    b, n_heads, ds, c = q_lat.shape
    past = cache_ckv.shape[2]
    n_split = 2 if n_heads % 2 == 0 else 1
    kern = functools.partial(_sattn_kernel, past=past, n_split=n_split)
    per_stream = lambda *blk: pl.BlockSpec((1,) + blk, lambda bb: (bb,) + (0,) * len(blk))
    rows = n_heads * ds // n_split
    return pl.pallas_call(
        kern,
        grid=(b,),
        in_specs=[per_stream(n_heads, ds, c), per_stream(n_heads, ds, 64),
                  pl.BlockSpec((None, 1, past, c), lambda bb: (layer, bb, 0, 0)),
                  pl.BlockSpec((None, 1, 64, past), lambda bb: (layer, bb, 0, 0)),
                  per_stream(ds, c), per_stream(ds, 64)],
        out_specs=per_stream(n_heads, ds, c),
        out_shape=jax.ShapeDtypeStruct((b, n_heads, ds, c), BF16),
        compiler_params=_cparams(("parallel",), (2 * 4 * past * (c + 128) + 2 * past * c + 12 * rows * past) // 2 ** 20 + 10),
        name="attn_sample",
    )(q_lat, q_pe, cache_ckv, cache_kpe_t, ckv_new, kpe_new)


def _absorb_q(qn, w_ukv3, layer, n_heads, b, ds):
    m = qn.shape[0]
    c = w_ukv3.shape[1]

    def epilogue(acc, ex, o):
        o[0][...] = acc.astype(BF16).reshape(b, ds, c)

    return _mm("absorb_q", (n_heads,), qn, pl.BlockSpec((m, 128), lambda h: (0, h)),
               w_ukv3, pl.BlockSpec((None, c, 128), lambda h: (layer, 0, 2 * h)), [],
               [(jax.ShapeDtypeStruct((b, n_heads, ds, c), BF16), pl.BlockSpec((b, None, ds, c), lambda h: (0, h, 0, 0)))],
               epilogue, trans_b=True, sem=("parallel",), vmem_mb=16)[0]


def _expand_v(o_lat, w_ukv3, layer):
    b, n_heads, ds, c = o_lat.shape
    m = b * ds

    def epilogue(acc, ex, o):
        o[0][...] = acc.astype(BF16)

    return _mm("expand_v", (n_heads,), o_lat, pl.BlockSpec((b, None, ds, c), lambda h: (0, h, 0, 0)),
               w_ukv3, pl.BlockSpec((None, c, 128), lambda h: (layer, 0, 2 * h + 1)), [],
               [(jax.ShapeDtypeStruct((m, n_heads * 128), BF16), pl.BlockSpec((m, 128), lambda h: (0, h)))],
               epilogue, sem=("parallel",), vmem_mb=16)[0]


def _pool_kernel(u_ref, prev_ref, w_ref, ps_ref, x_ref, g_ref, o_ref, st, *, ts, pos0, prev_rows, seq_tiles):
    grp = pl.program_id(0)
    sidx = pl.program_id(2)
    nb, gw = u_ref.shape[0], u_ref.shape[2]
    total = POOL_HALO + ts
    u = u_ref[...]
    if prev_rows == POOL_HALO:
        halo = prev_ref[...]
        st[:, 0:POOL_HALO, :] = jnp.where(sidx % seq_tiles == 0, jnp.zeros_like(halo), halo)
    else:
        st[:, 0:POOL_HALO, :] = jnp.zeros((nb, POOL_HALO, gw), F32)
        st[:, POOL_HALO - prev_rows:POOL_HALO, :] = prev_ref[...]
    st[:, POOL_HALO:total, :] = u
    pos = pos0 + (sidx % seq_tiles) * ts + lax.broadcasted_iota(jnp.int32, (nb, ts, gw), 1)

    def mix(k):
        s = st[...].reshape(nb * total, gw)
        for step in range(k + 1):
            s = s + pltpu.roll(s, 2 ** step, 0)
        win = s.reshape(nb, total, gw)[:, POOL_HALO:, :]
        cnt = jnp.minimum(pos + 1, POOL_WINDOWS[k]).astype(F32)
        d = (win / cnt - u).astype(BF16).reshape(nb * ts, gw)
        y = jnp.dot(d, w_ref[...].astype(BF16), preferred_element_type=F32).reshape(nb, ts, gw)
        o_ref[...] = x_ref[...] + g_ref[...] * (y * ps_ref[...])

    for k in range(len(POOL_WINDOWS)):
        pl.when(grp == k)(functools.partial(mix, k))


def _pool_mix(u, prev, w_pool, pool_scale, layer_j, x, mod, k_gate, pos0):
    b, s, d = u.shape
    ng = w_pool.shape[1]
    gw = d // ng
    nb, ts = _row_blocks(b, s)
    if prev is None:
        nb, ts = 1, _pick(s, 512, POOL_HALO)
        seq_tiles = s // ts
        per = ts // POOL_HALO
        prev_arr, prev_rows = u, POOL_HALO
        prev_spec = pl.BlockSpec((1, POOL_HALO, gw), lambda g, i, j: (i, jnp.maximum(j * per - 1, 0), g))
    else:
        nb = _pick(b, 8, 1)
        seq_tiles = 1
        prev_rows = prev.shape[2]
        prev_arr = prev
        prev_spec = pl.BlockSpec((None, nb, prev_rows, gw), lambda g, i, j: (layer_j, i, 0, g))
    kern = functools.partial(_pool_kernel, ts=ts, pos0=pos0, prev_rows=prev_rows, seq_tiles=seq_tiles)
    blk = pl.BlockSpec((nb, ts, gw), lambda g, i, j: (i, j, g))
    return pl.pallas_call(
        kern,
        grid=(ng, b // nb, s // ts),
        in_specs=[blk, prev_spec,
                  pl.BlockSpec((None, None, gw, gw), lambda g, i, j: (layer_j, g, 0, 0)),
                  pl.BlockSpec((None, 1, 1, gw), lambda g, i, j: (layer_j, 0, 0, g)),
                  blk,
                  pl.BlockSpec((nb, None, 1, gw), lambda g, i, j: (i, k_gate, 0, g))],
        out_specs=blk,
        out_shape=jax.ShapeDtypeStruct((b, s, d), F32),
        scratch_shapes=[pltpu.VMEM((nb, POOL_HALO + ts, gw), F32)],
        compiler_params=_cparams(("arbitrary", "arbitrary", "arbitrary"), 48),
        name="pool_mix",
    )(u, prev_arr, w_pool, pool_scale.reshape(pool_scale.shape[0], 1, 1, d), x, mod)


def _conv_gate(hs_g, hs_v, base, rows, cw_g, cw_v, cb_g, cb_v, seq_axis):
    def conv(hs, cw, cb):
        def sl(off):
            idx = [slice(None)] * len(hs.shape)
            idx[seq_axis] = slice(base - off, base - off + rows)
            return hs[tuple(idx)]
        return cb[...] + sl(2) * cw[0:1, :] + sl(1) * cw[1:2, :] + sl(0) * cw[2:3, :]
    return _silu(conv(hs_g, cw_g, cb_g)) * conv(hs_v, cw_v, cb_v)


def _up_prompt_kernel(u_ref, wg_ref, wv_ref, cwg_ref, cwv_ref, cbg_ref, cbv_ref, wd_ref,
                      act_ref, sg_ref, sv_ref, wdb_ref, *, tm, n_sub):
    wdb_ref[...] = wd_ref[...].astype(BF16)
    tf = wg_ref.shape[1]
    streams = ((wg_ref, cwg_ref, cbg_ref), (wv_ref, cwv_ref, cbv_ref))
    sub = lax.broadcasted_iota(jnp.int32, (8, tf), 0)
    tails = [[jnp.zeros((8, tf), F32)] * 2 for _ in streams]
    for r in range(n_sub):
        a = u_ref[0, r * tm:(r + 1) * tm, :]
        conv = []
        for si, (w_ref, cw, cb) in enumerate(streams):
            h = jnp.dot(a, w_ref[...], preferred_element_type=F32)
            acc = cb[...] + h * cw[2:3, :]
            for k in (1, 2):
                rk = pltpu.roll(h, k, 0)
                hk = jnp.concatenate([jnp.where(sub < k, tails[si][k - 1], rk[:8]), rk[8:]], axis=0)
                tails[si][k - 1] = rk[:8]
                acc = acc + hk * cw[2 - k:3 - k, :]
            conv.append(acc)
        act_ref[0, r * tm:(r + 1) * tm, :] = (_silu(conv[0]) * conv[1]).astype(BF16)
    sg_ref[0] = tails[0][1][0:2]
    sv_ref[0] = tails[1][1][0:2]


def _up_sample_kernel(u_ref, pg_ref, pv_ref, wg_ref, wv_ref, cwg_ref, cwv_ref, cbg_ref, cbv_ref,
                      act_ref, sg_ref, sv_ref, *rest, nb, ds, emit):
    hs_g, hs_v = rest[-2:]
    a = u_ref[...]
    for si, (w_ref, p_ref, hs, s_ref) in enumerate(((wg_ref, pg_ref, hs_g, sg_ref), (wv_ref, pv_ref, hs_v, sv_ref))):
        w = w_ref[...]
        if emit:
            w = w.astype(BF16)
            rest[si][...] = w
        h = jnp.dot(a, w, preferred_element_type=F32)
        hs[:, 6:8, :] = p_ref[...]
        hs[:, 8:8 + ds, :] = h.reshape(nb, ds, h.shape[-1])
        s_ref[...] = hs[:, 6 + ds:8 + ds, :]
    act = _conv_gate(hs_g, hs_v, 8, ds, cwg_ref, cwv_ref, cbg_ref, cbv_ref, 1)
    act_ref[...] = act.reshape(nb * ds, act.shape[-1]).astype(BF16)


def _ffn_up_sample(h, w_gate, w_value, conv_w, conv_b, layer, state):
    b, s, d = h.shape
    emit = w_gate.ndim == 3
    f = w_gate.shape[-1] // 2 if emit else w_gate.shape[-1]
    tf = _pick(f, 256, 128)
    nf = f // tf
    cb3 = conv_b.reshape(conv_b.shape[0], 1, 2 * f)
    g2 = lambda j: (layer, 0, j)
    v2 = lambda j: (layer, 0, nf + j)
    col = lambda j: (0, j)
    st_spec = pl.BlockSpec((b, 2, tf), lambda j: (0, 0, j))
    w_specs = ([pl.BlockSpec((None, d, tf), g2), pl.BlockSpec((None, d, tf), v2)] if emit
               else [pl.BlockSpec((d, tf), col), pl.BlockSpec((d, tf), col)])
    copies = [jax.ShapeDtypeStruct((d, f), BF16)] * 2 if emit else []
    outs = pl.pallas_call(
        functools.partial(_up_sample_kernel, nb=b, ds=s, emit=emit),
        grid=(nf,),
        in_specs=[pl.BlockSpec((b * s, d), lambda j: (0, 0)),
                  pl.BlockSpec((None, b, 2, tf), lambda j: (layer, 0, 0, j)),
                  pl.BlockSpec((None, b, 2, tf), lambda j: (layer, 0, 0, nf + j))]
                 + w_specs
                 + [pl.BlockSpec((None, 3, tf), g2), pl.BlockSpec((None, 3, tf), v2),
                    pl.BlockSpec((None, 1, tf), g2), pl.BlockSpec((None, 1, tf), v2)],
        out_specs=[pl.BlockSpec((b * s, tf), col), st_spec, st_spec] + [pl.BlockSpec((d, tf), col)] * len(copies),
        out_shape=[jax.ShapeDtypeStruct((b * s, f), BF16),
                   jax.ShapeDtypeStruct((b, 2, f), F32), jax.ShapeDtypeStruct((b, 2, f), F32)] + copies,
        scratch_shapes=[pltpu.VMEM((b, 8 + s, tf), F32), pltpu.VMEM((b, 8 + s, tf), F32)],
        compiler_params=_cparams(("arbitrary",), 40),
        name="ffn_up_sample",
    )(h.reshape(b * s, d), state, state, w_gate, w_value, conv_w, conv_w, cb3, cb3)
    return (outs[0], jnp.concatenate(outs[1:3], axis=-1)) + tuple(outs[3:])


def _ffn_up_prompt(h, wgb, wvb, conv_w, conv_b, w_down, layer):
    b, s, d = h.shape
    f = wgb.shape[1]
    tf = min(f, 512)
    n_col = pl.cdiv(f, tf)
    n_sub = 8 if s % 128 == 0 else 1
    tr = -(-pl.cdiv(f, n_col) // 16) * 16
    tc = d // b
    assert tc % 128 == 0 and (n_col - 1) * tr < f <= n_col * tr
    col = lambda bb, j: (0, j)
    st_spec = pl.BlockSpec((1, 2, tf), lambda bb, j: (bb, 0, j))
    cw, cb = conv_w[layer], conv_b[layer].reshape(1, 2 * f)
    act, sg, sv, w_down_b = pl.pallas_call(
        functools.partial(_up_prompt_kernel, tm=s // n_sub, n_sub=n_sub),
        grid=(b, n_col),
        in_specs=[pl.BlockSpec((1, s, d), lambda bb, j: (bb, 0, 0), pipeline_mode=pl.Buffered(1)),
                  pl.BlockSpec((d, tf), col), pl.BlockSpec((d, tf), col),
                  pl.BlockSpec((3, tf), col), pl.BlockSpec((3, tf), col),
                  pl.BlockSpec((1, tf), col), pl.BlockSpec((1, tf), col),
                  pl.BlockSpec((None, tr, tc), lambda bb, j: (layer, j, bb))],
        out_specs=[pl.BlockSpec((1, s, tf), lambda bb, j: (bb, 0, j)), st_spec, st_spec,
                   pl.BlockSpec((tr, tc), lambda bb, j: (j, bb))],
        out_shape=[jax.ShapeDtypeStruct((b, s, f), BF16),
                   jax.ShapeDtypeStruct((b, 2, f), F32), jax.ShapeDtypeStruct((b, 2, f), F32),
                   jax.ShapeDtypeStruct((f, d), BF16)],
        compiler_params=_cparams(("arbitrary", "arbitrary"),
                                 (2 * s * d + 8 * d * tf + 40 * s * tf + 12 * tr * tc) // 2 ** 20 + 8),
        name="ffn_up_prompt",
    )(h, wgb, wvb, cw[:, :f], cw[:, f:], cb[:, :f], cb[:, f:], w_down)
    return act.reshape(b * s, f), jnp.concatenate([sg, sv], axis=-1), w_down_b


def kernel(x_prompt, x_sample, c_prompt, c_sample, cache_ckv, cache_kpe, state_pool, state_conv, w_ada, b_ada,
           norm_mix_w, norm_ffn_w, w_dq, q_norm_w, w_uq, w_dkv, kv_norm_w, w_ukv, w_o, w_pool, pool_scale,
           w_up, conv_w, conv_b, w_down, final_norm_w):
    bp, s, d = x_prompt.shape
    bs, ds, _ = x_sample.shape
    past = cache_ckv.shape[2]
    depth = w_ada.shape[0]
    n_heads = w_uq.shape[2]
    q_lora = w_dq.shape[2]
    kv_lora = w_dkv.shape[2] - 64
    qk_head = w_uq.shape[3]
    scale = float(qk_head) ** -0.5
    pos_p = jnp.arange(s, dtype=jnp.int32)
    pos_s = jnp.tile(past + jnp.arange(ds, dtype=jnp.int32), bs)

    nb_all = bp + bs
    pad = (-nb_all) % 16
    c_all = jnp.concatenate([c_prompt, c_sample, jnp.zeros((pad, d), F32)], axis=0)
    mod_all = _ada(c_all, w_ada, b_ada)

    xp, xs = x_prompt, x_sample
    ckv_p_l, kpe_p_l, ckv_s_l, kpe_s_l = [], [], [], []
    pool_p_l, pool_s_l, conv_p_l, conv_s_l = [], [], [], []
    for i in range(depth):
        j = i // 2
        mod_p = mod_all[i, :bp].reshape(bp, 6, 1, d)
        mod_s = mod_all[i, bp:nb_all].reshape(bs, 6, 1, d)
        w_up_b = None
        if i % 2 == 0:
            w_in = jnp.concatenate([w_dq[j], w_dkv[j], jnp.zeros((d, 64), F32)], axis=1).astype(BF16)
            w_q_nope = w_uq[j][:, :, :128].reshape(q_lora, n_heads * 128).astype(BF16)
            w_q_pe = w_uq[j][:, :, 128:].reshape(q_lora, n_heads * 64).astype(BF16)
            w_ukv3 = w_ukv.reshape(w_ukv.shape[0], kv_lora, n_heads * 256)
            cq, ckv_p, ckv_pb, kpe_p, kpe_pb = _mla_project(xp, norm_mix_w, i, mod_p, pos_p, w_in, q_norm_w, kv_norm_w, j,
                                                            q_lora, kv_lora)
            qn, qp = _q_heads(cq, w_q_nope, w_q_pe, pos_p, scale)
            kn, vv = _kv_heads(ckv_pb, w_ukv3, j, n_heads)
            r3 = lambda t: t.reshape(bp, s, t.shape[-1])
            o_p, *w_up_b = _flash(r3(qn), r3(qp), r3(kn), r3(vv), r3(kpe_pb), n_heads, w_up, i)
            xp = _proj_residual("attn_out_prompt", o_p.reshape(bp * s, n_heads * 128), w_o, j, xp, mod_p, 2)
            cq, ckv_s, _, kpe_s, _ = _mla_project(xs, norm_mix_w, i, mod_s, pos_s, w_in, q_norm_w, kv_norm_w, j,
                                                  q_lora, kv_lora)
            qn, qp = _q_heads(cq, w_q_nope, w_q_pe, pos_s, scale)
            q_lat = _absorb_q(qn, w_ukv3, j, n_heads, bs, ds)
            qp_heads = qp.reshape(bs, ds, n_heads, 64).transpose(0, 2, 1, 3)
            o_lat = _sample_attention(q_lat, qp_heads, cache_ckv, jnp.swapaxes(cache_kpe, 2, 3), j, ckv_s.reshape(bs, ds, kv_lora),
                                      kpe_s.reshape(bs, ds, 64))
            o_s = _expand_v(o_lat, w_ukv3, j)
            xs = _proj_residual("attn_out_sample", o_s, w_o, j, xs, mod_s, 2)
            ckv_p_l.append(ckv_p.reshape(bp, s, kv_lora))
            kpe_p_l.append(kpe_p.reshape(bp, s, 64))
            ckv_s_l.append(ckv_s.reshape(bs, ds, kv_lora))
            kpe_s_l.append(kpe_s.reshape(bs, ds, 64))
        else:
            hist = state_pool.shape[2]
            up = _normmod(xp, norm_mix_w, i, mod_p, 0, 1, F32)
            us = _normmod(xs, norm_mix_w, i, mod_s, 0, 1, F32)
            pool_p_l.append(up[:, s - hist:])
            pool_s_l.append(jnp.concatenate([state_pool[j], us], axis=1)[:, -hist:])
            xp = _pool_mix(up, None, w_pool, pool_scale, j, xp, mod_p, 2, 0)
            xs = _pool_mix(us, state_pool, w_pool, pool_scale, j, xs, mod_s, 2, past)
        hp = _normmod(xp, norm_ffn_w, i, mod_p, 3, 4, BF16)
        hs = _normmod(xs, norm_ffn_w, i, mod_s, 3, 4, BF16)
        if w_up_b is None:
            act_s, cst_s, *w_up_b = _ffn_up_sample(hs, w_up, w_up, conv_w, conv_b, i, state_conv)
        else:
            act_s, cst_s = _ffn_up_sample(hs, *w_up_b, conv_w, conv_b, i, state_conv)
        act_p, cst_p, w_down_b = _ffn_up_prompt(hp, *w_up_b, conv_w, conv_b, w_down, i)
        conv_p_l.append(cst_p)
        conv_s_l.append(cst_s)
        xs = _proj_residual("ffn_down_sample", act_s, w_down_b, None, xs, mod_s, 5)
        xp = _proj_residual("ffn_down_prompt", act_p, w_down_b, None, xp, mod_p, 5)

    y_prompt = _final_norm(xp, final_norm_w)
    y_sample = _final_norm(xs, final_norm_w)
    return (y_prompt, y_sample,
            jnp.stack(ckv_p_l), jnp.stack(kpe_p_l), jnp.stack(ckv_s_l), jnp.stack(kpe_s_l),
            jnp.stack(pool_p_l), jnp.stack(pool_s_l), jnp.stack(conv_p_l), jnp.stack(conv_s_l))
```

```python
import functools

import jax
import jax.numpy as jnp
from jax import lax
from jax.experimental import pallas as pl
from jax.experimental.pallas import tpu as pltpu

F32 = jnp.float32
BF16 = jnp.bfloat16

CHUNK = 64
ROPE_THETA = 10000.0
RMS_EPS = 1e-6
NEG_INF = -1e30
POOL_WINDOWS = (2, 4, 8, 16)
POOL_HALO = 32
V7X_VMEM_CAP_MB = 56


def _cparams(sem, vmem_mb, flags=None):
    return pltpu.CompilerParams(dimension_semantics=sem, vmem_limit_bytes=min(vmem_mb, V7X_VMEM_CAP_MB) * 2 ** 20,
                                flags=flags)


def _pick(dim, pref, align):
    t = min(pref, dim)
    t -= t % align
    while t >= align:
        if dim % t == 0:
            return t
        t -= align
    return dim


def _silu(x):
    return x * (0.5 + 0.5 * jnp.tanh(0.5 * x))


def _rms(x, w):
    return x * lax.rsqrt(jnp.mean(x * x, axis=-1, keepdims=True) + RMS_EPS) * w


def _rope_rot(x):
    n = x.shape[-1]
    lane = lax.broadcasted_iota(jnp.int32, x.shape, x.ndim - 1)
    return jnp.where(lane % 64 < 32, -pltpu.roll(x, n - 32, x.ndim - 1), pltpu.roll(x, 32, x.ndim - 1))


def _rope(x, cos_ref, sin_ref):
    reps = x.shape[-1] // 128
    c = cos_ref[...]
    s = sin_ref[...]
    if reps > 1:
        c = jnp.concatenate([c] * reps, axis=-1)
        s = jnp.concatenate([s] * reps, axis=-1)
    return x * c + _rope_rot(x) * s


def _ada_kernel(c_ref, w_ref, b_ref, o_ref):
    a = _silu(c_ref[...]).astype(BF16)
    o_ref[...] = jnp.dot(a, w_ref[...].astype(BF16), preferred_element_type=F32) + b_ref[...]


def _ada(c, w_ada, b_ada):
    n_layers, d, n = w_ada.shape
    bc = c.shape[0]
    tn = _pick(n, 512, 128)
    return pl.pallas_call(
        _ada_kernel,
        grid=(n_layers, n // tn),
        in_specs=[pl.BlockSpec((bc, d), lambda l, j: (0, 0)),
                  pl.BlockSpec((None, d, tn), lambda l, j: (l, 0, j)),
                  pl.BlockSpec((None, 1, tn), lambda l, j: (l, 0, j))],
        out_specs=pl.BlockSpec((None, bc, tn), lambda l, j: (l, 0, j)),
        out_shape=jax.ShapeDtypeStruct((n_layers, bc, n), F32),
        compiler_params=_cparams(("parallel", "parallel"), 4 * d * tn * 3 // 2 ** 20 + 8),
        name="ada",
    )(c, w_ada, b_ada.reshape(n_layers, 1, n))


def _normmod_kernel(x_ref, w_ref, sh_ref, sc_ref, o_ref):
    y = _rms(x_ref[...], w_ref[...])
    o_ref[...] = (y * (1.0 + sc_ref[...]) + sh_ref[...]).astype(o_ref.dtype)


def _norm_kernel(x_ref, w_ref, o_ref):
    o_ref[...] = _rms(x_ref[...], w_ref[...]).astype(o_ref.dtype)


def _row_blocks(b, s):
    if s >= 512:
        return 1, _pick(s, 512, 16)
    return _pick(b, max(1, 512 // s), 1), s


def _norm_vmem_mb(block_elems, out_dtype):
    return block_elems * (2 * 4 + 2 * jnp.dtype(out_dtype).itemsize + 2 * 4) // 2 ** 20 + 4


def _normmod(x, w, layer, mod, k_shift, k_scale, out_dtype):
    b, s, d = x.shape
    nb, ts = _row_blocks(b, s)
    w3 = w.reshape(w.shape[0], 1, d)
    return pl.pallas_call(
        _normmod_kernel,
        grid=(b // nb, s // ts),
        in_specs=[pl.BlockSpec((nb, ts, d), lambda i, j: (i, j, 0)),
                  pl.BlockSpec((1, 1, d), lambda i, j: (layer, 0, 0)),
                  pl.BlockSpec((nb, None, 1, d), lambda i, j: (i, k_shift, 0, 0)),
                  pl.BlockSpec((nb, None, 1, d), lambda i, j: (i, k_scale, 0, 0))],
        out_specs=pl.BlockSpec((nb, ts, d), lambda i, j: (i, j, 0)),
        out_shape=jax.ShapeDtypeStruct((b, s, d), out_dtype),
        compiler_params=_cparams(("parallel", "parallel"), _norm_vmem_mb(nb * ts * d, out_dtype)),
        name="normmod",
    )(x, w3, mod, mod)


def _final_norm(x, w):
    b, s, d = x.shape
    nb, ts = _row_blocks(b, s)
    return pl.pallas_call(
        _norm_kernel,
        grid=(b // nb, s // ts),
        in_specs=[pl.BlockSpec((nb, ts, d), lambda i, j: (i, j, 0)),
                  pl.BlockSpec((1, 1, d), lambda i, j: (0, 0, 0))],
        out_specs=pl.BlockSpec((nb, ts, d), lambda i, j: (i, j, 0)),
        out_shape=jax.ShapeDtypeStruct((b, s, d), F32),
        compiler_params=_cparams(("parallel", "parallel"), _norm_vmem_mb(nb * ts * d, F32)),
        name="final_norm",
    )(x, w.reshape(1, 1, d))


def _mm(name, grid, a, a_spec, b, b_spec, extras, outs, epilogue, *, prologue=None, trans_b=False, sem, vmem_mb):
    n_ex, n_out = len(extras), len(outs)
    dn = (((1,), (1 if trans_b else 0,)), ((), ()))

    def kern(*refs):
        ex = refs[2:2 + n_ex]
        a_blk = refs[0][...]
        if prologue is not None:
            a_blk = prologue(a_blk, ex)
        a_blk = a_blk.reshape(-1, a_blk.shape[-1]).astype(BF16)
        acc = lax.dot_general(a_blk, refs[1][...].astype(BF16), dn, preferred_element_type=F32)
        epilogue(acc, ex, refs[2 + n_ex:2 + n_ex + n_out])

    return pl.pallas_call(
        kern,
        grid=grid,
        in_specs=[a_spec, b_spec] + [s for _, s in extras],
        out_specs=[s for _, s in outs],
        out_shape=[sd for sd, _ in outs],
        compiler_params=_cparams(sem, vmem_mb),
        name=name,
    )(a, b, *[x for x, _ in extras])


def _rope_tables(pos):
    half = 32
    inv = jnp.power(ROPE_THETA, -jnp.arange(half, dtype=F32) / half)
    ang = pos.astype(F32)[:, None] * inv[None, :]
    return jnp.tile(jnp.cos(ang), (1, 4)), jnp.tile(jnp.sin(ang), (1, 4))


def _mla_project(x, norm_w, norm_layer, mod, pos_rows, w_in, q_norm_w, kv_norm_w, layer, q_lora, kv_lora):
    b, s, d = x.shape
    m = b * s
    n = w_in.shape[1]
    period = pos_rows.shape[0]
    tm = _pick(period, 512, 16)
    cos, sin = _rope_tables(pos_rows)
    npb = period // tm
    if s % tm == 0:
        per = s // tm
        mods = [(mod, pl.BlockSpec((None, None, 1, d), lambda i, k=k: (i // per, k, 0, 0))) for k in (0, 1)]
    else:
        mods = [(jnp.repeat(mod[:, k, 0, :], s, axis=0), pl.BlockSpec((tm, d), lambda i: (i, 0))) for k in (0, 1)]

    def prologue(a, ex):
        return _rms(a, ex[0][...]) * (1.0 + ex[2][...]) + ex[1][...]

    def epilogue(acc, ex, o):
        qw_ref, kw_ref, cos_ref, sin_ref = ex[3:]
        o[0][...] = _rms(acc[:, :q_lora], qw_ref[...]).astype(BF16)
        ckv = _rms(acc[:, q_lora:q_lora + kv_lora], kw_ref[...])
        o[1][...] = ckv
        o[2][...] = ckv.astype(BF16)
        kpe = _rope(acc[:, q_lora + kv_lora:], cos_ref, sin_ref)[:, :64]
        o[3][...] = kpe
        o[4][...] = kpe.astype(BF16)

    row = lambda width: pl.BlockSpec((tm, width), lambda i: (i, 0))
    return _mm(
        "mla_in", (m // tm,), x.reshape(m, d), pl.BlockSpec((tm, d), lambda i: (i, 0)),
        w_in, pl.BlockSpec((d, n), lambda i: (0, 0), pipeline_mode=pl.Buffered(1)),
        [(norm_w.reshape(-1, 1, d), pl.BlockSpec((None, 1, d), lambda i: (norm_layer, 0, 0)))] + mods +
        [(q_norm_w.reshape(-1, 1, q_lora), pl.BlockSpec((None, 1, q_lora), lambda i: (layer, 0, 0))),
         (kv_norm_w.reshape(-1, 1, kv_lora), pl.BlockSpec((None, 1, kv_lora), lambda i: (layer, 0, 0))),
         (cos, pl.BlockSpec((tm, 128), lambda i: (i % npb, 0))),
         (sin, pl.BlockSpec((tm, 128), lambda i: (i % npb, 0)))],
        [(jax.ShapeDtypeStruct((m, q_lora), BF16), row(q_lora)),
         (jax.ShapeDtypeStruct((m, kv_lora), F32), row(kv_lora)),
         (jax.ShapeDtypeStruct((m, kv_lora), BF16), row(kv_lora)),
         (jax.ShapeDtypeStruct((m, 64), F32), row(64)),
         (jax.ShapeDtypeStruct((m, 64), BF16), row(64))],
        epilogue, prologue=prologue, sem=("parallel",),
        vmem_mb=(2 * d * n + (8 + 12 + (0 if s % tm == 0 else 16)) * tm * d + 16 * tm * n) // 2 ** 20 + 8)


def _q_heads(cq, w_nope, w_pe, pos_rows, scale):
    m, kq = cq.shape
    period = pos_rows.shape[0]
    tm = _pick(period, 512, 16)
    npb = period // tm
    cos, sin = _rope_tables(pos_rows)

    def plain(acc, ex, o):
        o[0][...] = (acc * scale).astype(BF16)

    def roped(acc, ex, o):
        o[0][...] = (_rope(acc, ex[0], ex[1]) * scale).astype(BF16)

    outs = []
    for name, w, extras, epi in (("q_nope", w_nope, [], plain),
                                 ("q_pe", w_pe, [(cos, pl.BlockSpec((tm, 128), lambda i, j: (i % npb, 0))),
                                                 (sin, pl.BlockSpec((tm, 128), lambda i, j: (i % npb, 0)))], roped)):
        n = w.shape[1]
        tn = _pick(n, 2048, 128)
        outs.append(_mm(name, (m // tm, n // tn), cq, pl.BlockSpec((tm, kq), lambda i, j: (i, 0)),
                        w, pl.BlockSpec((kq, tn), lambda i, j: (0, j)), extras,
                        [(jax.ShapeDtypeStruct((m, n), BF16), pl.BlockSpec((tm, tn), lambda i, j: (i, j)))],
                        epi, sem=("parallel", "parallel"), vmem_mb=40)[0])
    return outs


def _kv_heads(ckv_b, w_ukv3, layer, n_heads):
    m, kc = ckv_b.shape
    tm = _pick(m, 512, 16)
    hg = 4 if n_heads % 4 == 0 else 1

    def kern(c_ref, w_ref, k_ref, v_ref):
        c = c_ref[...]
        for g in range(n_heads // hg):
            acc = jnp.dot(c, w_ref[:, g * hg * 256:(g + 1) * hg * 256].astype(BF16), preferred_element_type=F32)
            for h in range(hg):
                cols = slice((g * hg + h) * 128, (g * hg + h + 1) * 128)
                k_ref[:, cols] = acc[:, h * 256:h * 256 + 128].astype(BF16)
                v_ref[:, cols] = acc[:, h * 256 + 128:(h + 1) * 256].astype(BF16)

    out = jax.ShapeDtypeStruct((m, n_heads * 128), BF16)
    out_spec = pl.BlockSpec((tm, n_heads * 128), lambda i: (i, 0))
    return pl.pallas_call(
        kern,
        grid=(m // tm,),
        in_specs=[pl.BlockSpec((tm, kc), lambda i: (i, 0)),
                  pl.BlockSpec((None, kc, n_heads * 256), lambda i: (layer, 0, 0), pipeline_mode=pl.Buffered(1))],
        out_specs=[out_spec, out_spec],
        out_shape=[out, out],
        compiler_params=_cparams(("parallel",), (4 * kc * n_heads * 256 + 8 * tm * n_heads * 128) // 2 ** 20 + 16),
        name="kv_heads",
    )(ckv_b, w_ukv3)


def _gate_specs(x, mod, k_gate, tm, tn):
    b, s, d = x.shape
    x2 = x.reshape(b * s, d)
    x_spec = pl.BlockSpec((tm, tn), lambda i, j: (i, j))
    if s % tm == 0:
        per = s // tm
        g = (mod, pl.BlockSpec((None, None, 1, tn), lambda i, j: (i // per, k_gate, 0, j)))
    else:
        rows = jnp.repeat(mod[:, k_gate, 0, :], s, axis=0)
        g = (rows, pl.BlockSpec((tm, tn), lambda i, j: (i, j)))
    return [(x2, x_spec), g]


LHS_TILE_BYTES = 12 * 2 ** 20


def _proj_residual(name, a2, w, layer, x, mod, k_gate):
    m, kk = a2.shape
    n = w.shape[-1]
    b, s, _ = x.shape
    rows = min(1024, LHS_TILE_BYTES // (2 * kk))
    tm = _pick(s, rows, 16) if s >= 256 else _pick(m, rows, 16)
    tn = _pick(n, 512, 128)
    w_spec = (pl.BlockSpec((None, kk, tn), lambda i, j: (layer, 0, j)) if w.ndim == 3
              else pl.BlockSpec((kk, tn), lambda i, j: (0, j)))

    def epilogue(acc, ex, o):
        o[0][...] = ex[0][...] + ex[1][...] * acc

    out = _mm(name, (m // tm, n // tn), a2, pl.BlockSpec((tm, kk), lambda i, j: (i, 0)),
              w, w_spec, _gate_specs(x, mod, k_gate, tm, tn),
              [(jax.ShapeDtypeStruct((m, n), F32), pl.BlockSpec((tm, tn), lambda i, j: (i, j)))],
              epilogue, sem=("parallel", "arbitrary"),
              vmem_mb=(4 * tm * kk + (10 if w.ndim == 3 else 4) * kk * tn + 36 * tm * tn) // 2 ** 20 + 8)[0]
    return out.reshape(b, s, n)


def _flash_kernel(qn_ref, qp_ref, kn_ref, v_ref, kpe_ref, wg_ref, wv_ref, o_ref, wgb_ref, wvb_ref, *, t, nq, n_cast):
    @pl.when(pl.program_id(0) * pl.num_programs(1) + pl.program_id(1) < n_cast)
    def _():
        wgb_ref[...] = wg_ref[...].astype(BF16)
        wvb_ref[...] = wv_ref[...].astype(BF16)

    nt = (((1,), (1,)), ((), ()))
    kpe = kpe_ref[0]
    qc = lax.broadcasted_iota(jnp.int32, (t, t), 0) // CHUNK
    kc = lax.broadcasted_iota(jnp.int32, (t, t), 1) // CHUNK
    diag_visible = kc <= qc
    for hh in range(2):
        hs = slice(hh * 128, (hh + 1) * 128)
        kcat = jnp.concatenate([kn_ref[0, :, hs], kpe], axis=-1)
        for qi in range(nq):
            lo = qi * t
            rows = slice(lo, lo + t)
            q = jnp.concatenate([qn_ref[0, rows, hs], qp_ref[0, rows, hh * 64:(hh + 1) * 64]], axis=-1)
            s_d = jnp.where(diag_visible, lax.dot_general(q, kcat[lo:lo + t], nt, preferred_element_type=F32), NEG_INF)
            m = jnp.max(s_d, axis=-1, keepdims=True)
            if qi > 0:
                s_o = lax.dot_general(q, kcat[:lo], nt, preferred_element_type=F32)
                m = jnp.maximum(m, jnp.max(s_o, axis=-1, keepdims=True))
                p_o = jnp.exp(s_o - m)
            p_d = jnp.exp(s_d - m)
            l = jnp.sum(p_d, axis=-1, keepdims=True)
            acc = jnp.dot(p_d.astype(BF16), v_ref[0, rows, hs], preferred_element_type=F32)
            if qi > 0:
                l = l + jnp.sum(p_o, axis=-1, keepdims=True)
                acc = acc + jnp.dot(p_o.astype(BF16), v_ref[0, 0:lo, hs], preferred_element_type=F32)
            o_ref[0, rows, hs] = (acc / l).astype(BF16)


def _flash(qn, qp, kn, v, kpe, n_heads, w_up, layer):
    b, s, _ = qn.shape
    d, f = w_up.shape[1], w_up.shape[2] // 2
    t = _pick(s, 512, CHUNK)
    tf = _pick(f, 256, 128)
    n_cast = f // tf
    hp_steps = n_heads // 2
    assert n_cast <= b * hp_steps
    cast_tile = lambda bb, hp: jnp.minimum(bb * hp_steps + hp, n_cast - 1)
    kern = functools.partial(_flash_kernel, t=t, nq=s // t, n_cast=n_cast)
    wide = pl.BlockSpec((1, s, 256), lambda bb, hp: (bb, 0, hp))
    wb_spec = pl.BlockSpec((d, tf), lambda bb, hp: (0, cast_tile(bb, hp)))
    return pl.pallas_call(
        kern,
        grid=(b, hp_steps),
        in_specs=[wide, pl.BlockSpec((1, s, 128), lambda bb, hp: (bb, 0, hp)), wide, wide,
                  pl.BlockSpec((1, s, 64), lambda bb, hp: (bb, 0, 0)),
                  pl.BlockSpec((None, d, tf), lambda bb, hp: (layer, 0, cast_tile(bb, hp))),
                  pl.BlockSpec((None, d, tf), lambda bb, hp: (layer, 0, n_cast + cast_tile(bb, hp)))],
        out_specs=[wide, wb_spec, wb_spec],
        out_shape=[jax.ShapeDtypeStruct((b, s, n_heads * 128), BF16),
                   jax.ShapeDtypeStruct((d, f), BF16), jax.ShapeDtypeStruct((d, f), BF16)],
        compiler_params=_cparams(("arbitrary", "arbitrary"), 30 + 24 * d * tf // 2 ** 20),
        name="flash_prompt",
    )(qn, qp, kn, v, kpe, w_up, w_up)


def _sattn_kernel(ql_ref, qp_ref, ck_ref, kp_ref, nck_ref, nkp_ref, o_ref, *, past, n_split):
    nt = (((1,), (1,)), ((), ()))
    n_heads, ds, c = ql_ref.shape[1:]
    ck = ck_ref[0].astype(BF16)
    kp = kp_ref[0].astype(BF16)
    nck = nck_ref[0].astype(BF16)
    nkp = nkp_ref[0].astype(BF16)
    hg = n_heads // n_split
    rows = hg * ds
    qpos = past + lax.broadcasted_iota(jnp.int32, (rows, ds), 0) % ds
    kpos = past + lax.broadcasted_iota(jnp.int32, (rows, ds), 1)
    new_visible = kpos // CHUNK <= qpos // CHUNK
    for g in range(n_split):
        ql = ql_ref[0, g * hg:(g + 1) * hg].reshape(rows, c)
        qp = qp_ref[0, g * hg:(g + 1) * hg].reshape(rows, qp_ref.shape[-1])
        s_c = lax.dot_general(ql, ck, nt, preferred_element_type=F32) + jnp.dot(qp, kp, preferred_element_type=F32)
        s_n = lax.dot_general(ql, nck, nt, preferred_element_type=F32) + lax.dot_general(qp, nkp, nt, preferred_element_type=F32)
        s_n = jnp.where(new_visible, s_n, NEG_INF)
        m = jnp.maximum(jnp.max(s_c, axis=-1, keepdims=True), jnp.max(s_n, axis=-1, keepdims=True))
        p_c = jnp.exp(s_c - m)
        p_n = jnp.exp(s_n - m)
        l = jnp.sum(p_c, axis=-1, keepdims=True) + jnp.sum(p_n, axis=-1, keepdims=True)
        acc = (jnp.dot(p_c.astype(BF16), ck, preferred_element_type=F32)
               + jnp.dot(p_n.astype(BF16), nck, preferred_element_type=F32))
        o_ref[0, g * hg:(g + 1) * hg] = (acc / l).astype(BF16).reshape(hg, ds, c)


def _sample_attention(q_lat, q_pe, cache_ckv, cache_kpe_t, layer, ckv_new, kpe_new):
    b, n_heads, ds, c = q_lat.shape
    past = cache_ckv.shape[2]
    n_split = 2 if n_heads % 2 == 0 else 1
    kern = functools.partial(_sattn_kernel, past=past, n_split=n_split)
    per_stream = lambda *blk: pl.BlockSpec((1,) + blk, lambda bb: (bb,) + (0,) * len(blk))
    rows = n_heads * ds // n_split
    return pl.pallas_call(
        kern,
        grid=(b,),
        in_specs=[per_stream(n_heads, ds, c), per_stream(n_heads, ds, 64),
                  pl.BlockSpec((None, 1, past, c), lambda bb: (layer, bb, 0, 0)),
                  pl.BlockSpec((None, 1, 64, past), lambda bb: (layer, bb, 0, 0)),
                  per_stream(ds, c), per_stream(ds, 64)],
        out_specs=per_stream(n_heads, ds, c),
        out_shape=jax.ShapeDtypeStruct((b, n_heads, ds, c), BF16),
        compiler_params=_cparams(("parallel",), (2 * 4 * past * (c + 128) + 2 * past * c + 12 * rows * past) // 2 ** 20 + 10),
        name="attn_sample",
    )(q_lat, q_pe, cache_ckv, cache_kpe_t, ckv_new, kpe_new)


def _absorb_q(qn, w_ukv3, layer, n_heads, b, ds):
    m = qn.shape[0]
    c = w_ukv3.shape[1]

    def epilogue(acc, ex, o):
        o[0][...] = acc.astype(BF16).reshape(b, ds, c)

    return _mm("absorb_q", (n_heads,), qn, pl.BlockSpec((m, 128), lambda h: (0, h)),
               w_ukv3, pl.BlockSpec((None, c, 128), lambda h: (layer, 0, 2 * h)), [],
               [(jax.ShapeDtypeStruct((b, n_heads, ds, c), BF16), pl.BlockSpec((b, None, ds, c), lambda h: (0, h, 0, 0)))],
               epilogue, trans_b=True, sem=("parallel",), vmem_mb=16)[0]


def _expand_v(o_lat, w_ukv3, layer):
    b, n_heads, ds, c = o_lat.shape
    m = b * ds

    def epilogue(acc, ex, o):
        o[0][...] = acc.astype(BF16)

    return _mm("expand_v", (n_heads,), o_lat, pl.BlockSpec((b, None, ds, c), lambda h: (0, h, 0, 0)),
               w_ukv3, pl.BlockSpec((None, c, 128), lambda h: (layer, 0, 2 * h + 1)), [],
               [(jax.ShapeDtypeStruct((m, n_heads * 128), BF16), pl.BlockSpec((m, 128), lambda h: (0, h)))],
               epilogue, sem=("parallel",), vmem_mb=16)[0]


def _pool_kernel(u_ref, prev_ref, w_ref, ps_ref, x_ref, g_ref, o_ref, st, *, ts, pos0, prev_rows, seq_tiles):
    grp = pl.program_id(0)
    sidx = pl.program_id(2)
    nb, gw = u_ref.shape[0], u_ref.shape[2]
    total = POOL_HALO + ts
    u = u_ref[...]
    if prev_rows == POOL_HALO:
        halo = prev_ref[...]
        st[:, 0:POOL_HALO, :] = jnp.where(sidx % seq_tiles == 0, jnp.zeros_like(halo), halo)
    else:
        st[:, 0:POOL_HALO, :] = jnp.zeros((nb, POOL_HALO, gw), F32)
        st[:, POOL_HALO - prev_rows:POOL_HALO, :] = prev_ref[...]
    st[:, POOL_HALO:total, :] = u
    pos = pos0 + (sidx % seq_tiles) * ts + lax.broadcasted_iota(jnp.int32, (nb, ts, gw), 1)

    def mix(k):
        s = st[...].reshape(nb * total, gw)
        for step in range(k + 1):
            s = s + pltpu.roll(s, 2 ** step, 0)
        win = s.reshape(nb, total, gw)[:, POOL_HALO:, :]
        cnt = jnp.minimum(pos + 1, POOL_WINDOWS[k]).astype(F32)
        d = (win / cnt - u).astype(BF16).reshape(nb * ts, gw)
        y = jnp.dot(d, w_ref[...].astype(BF16), preferred_element_type=F32).reshape(nb, ts, gw)
        o_ref[...] = x_ref[...] + g_ref[...] * (y * ps_ref[...])

    for k in range(len(POOL_WINDOWS)):
        pl.when(grp == k)(functools.partial(mix, k))


def _pool_mix(u, prev, w_pool, pool_scale, layer_j, x, mod, k_gate, pos0):
    b, s, d = u.shape
    ng = w_pool.shape[1]
    gw = d // ng
    nb, ts = _row_blocks(b, s)
    if prev is None:
        nb, ts = 1, _pick(s, 512, POOL_HALO)
        seq_tiles = s // ts
        per = ts // POOL_HALO
        prev_arr, prev_rows = u, POOL_HALO
        prev_spec = pl.BlockSpec((1, POOL_HALO, gw), lambda g, i, j: (i, jnp.maximum(j * per - 1, 0), g))
    else:
        nb = _pick(b, 8, 1)
        seq_tiles = 1
        prev_rows = prev.shape[2]
        prev_arr = prev
        prev_spec = pl.BlockSpec((None, nb, prev_rows, gw), lambda g, i, j: (layer_j, i, 0, g))
    kern = functools.partial(_pool_kernel, ts=ts, pos0=pos0, prev_rows=prev_rows, seq_tiles=seq_tiles)
    blk = pl.BlockSpec((nb, ts, gw), lambda g, i, j: (i, j, g))
    return pl.pallas_call(
        kern,
        grid=(ng, b // nb, s // ts),
        in_specs=[blk, prev_spec,
                  pl.BlockSpec((None, None, gw, gw), lambda g, i, j: (layer_j, g, 0, 0)),
                  pl.BlockSpec((None, 1, 1, gw), lambda g, i, j: (layer_j, 0, 0, g)),
                  blk,
                  pl.BlockSpec((nb, None, 1, gw), lambda g, i, j: (i, k_gate, 0, g))],
        out_specs=blk,
        out_shape=jax.ShapeDtypeStruct((b, s, d), F32),
        scratch_shapes=[pltpu.VMEM((nb, POOL_HALO + ts, gw), F32)],
        compiler_params=_cparams(("arbitrary", "arbitrary", "arbitrary"), 48),
        name="pool_mix",
    )(u, prev_arr, w_pool, pool_scale.reshape(pool_scale.shape[0], 1, 1, d), x, mod)


def _pool_start_kernel(x_ref, xh_ref, nw_ref, sh_ref, sc_ref, w_ref, ps_ref, g_ref, o_ref, tail_ref, wb_ref, *, ts):
    step = pl.program_id(1)

    @pl.when((pl.program_id(0) == 0) & (step == 0))
    def _():
        wb_ref[...] = w_ref[...].astype(BF16)

    def mixer_input(v):
        return _rms(v, nw_ref[...]) * (1.0 + sc_ref[...]) + sh_ref[...]

    x = x_ref[0]
    u = mixer_input(x)
    halo = mixer_input(xh_ref[0])
    ext = jnp.concatenate([jnp.where(step == 0, jnp.zeros_like(halo), halo), u], axis=0)
    tail_ref[0] = u[ts - tail_ref.shape[1]:, :]
    gw = w_ref.shape[-1]
    pos = step * ts + lax.broadcasted_iota(jnp.int32, (ts, gw), 0)
    for k, window in enumerate(POOL_WINDOWS):
        cols = slice(k * gw, (k + 1) * gw)
        s = ext[:, cols]
        for stage in range(k + 1):
            s = s + pltpu.roll(s, 2 ** stage, 0)
        cnt = jnp.minimum(pos + 1, window).astype(F32)
        d = (s[POOL_HALO:, :] / cnt - u[:, cols]).astype(BF16)
        y = jnp.dot(d, wb_ref[k], preferred_element_type=F32)
        o_ref[0, :, cols] = x[:, cols] + g_ref[:, cols] * (y * ps_ref[:, cols])


def _pool_mix_start(x, norm_w, norm_layer, mod, w_pool, pool_scale, layer_j):
    b, s, d = x.shape
    ng, gw = w_pool.shape[1], w_pool.shape[2]
    ts = _pick(s, 256, POOL_HALO)
    per = ts // POOL_HALO
    tail = 16
    mod_row = lambda k: pl.BlockSpec((None, None, 1, d), lambda i, j: (i, k, 0, 0))
    tile = pl.BlockSpec((1, ts, d), lambda i, j: (i, j, 0))
    return pl.pallas_call(
        functools.partial(_pool_start_kernel, ts=ts),
        grid=(b, s // ts),
        in_specs=[tile,
                  pl.BlockSpec((1, POOL_HALO, d), lambda i, j: (i, jnp.maximum(j * per - 1, 0), 0)),
                  pl.BlockSpec((None, 1, d), lambda i, j: (norm_layer, 0, 0)),
                  mod_row(0), mod_row(1),
                  pl.BlockSpec((None, ng, gw, gw), lambda i, j: (layer_j, 0, 0, 0), pipeline_mode=pl.Buffered(1)),
                  pl.BlockSpec((None, 1, d), lambda i, j: (layer_j, 0, 0)),
                  mod_row(2)],
        out_specs=[tile, pl.BlockSpec((1, tail, d), lambda i, j: (i, 0, 0))],
        out_shape=[jax.ShapeDtypeStruct((b, s, d), F32), jax.ShapeDtypeStruct((b, tail, d), F32)],
        scratch_shapes=[pltpu.VMEM((ng, gw, gw), BF16)],
        compiler_params=_cparams(("arbitrary", "arbitrary"), (6 * ng * gw * gw + 40 * ts * d) // 2 ** 20 + 8),
        name="pool_mix_start",
    )(x, x, norm_w.reshape(-1, 1, d), mod, mod, w_pool, pool_scale.reshape(-1, 1, d), mod)


def _conv_gate(hs_g, hs_v, base, rows, cw_g, cw_v, cb_g, cb_v, seq_axis):
    def conv(hs, cw, cb):
        def sl(off):
            idx = [slice(None)] * len(hs.shape)
            idx[seq_axis] = slice(base - off, base - off + rows)
            return hs[tuple(idx)]
        return cb[...] + sl(2) * cw[0:1, :] + sl(1) * cw[1:2, :] + sl(0) * cw[2:3, :]
    return _silu(conv(hs_g, cw_g, cb_g)) * conv(hs_v, cw_v, cb_v)


def _up_prompt_kernel(u_ref, wg_ref, wv_ref, cwg_ref, cwv_ref, cbg_ref, cbv_ref, wd_ref,
                      act_ref, sg_ref, sv_ref, wdb_ref, *, tm, n_sub):
    wdb_ref[...] = wd_ref[...].astype(BF16)
    tf = wg_ref.shape[1]
    streams = ((wg_ref, cwg_ref, cbg_ref), (wv_ref, cwv_ref, cbv_ref))
    sub = lax.broadcasted_iota(jnp.int32, (8, tf), 0)
    tails = [[jnp.zeros((8, tf), F32)] * 2 for _ in streams]
    for r in range(n_sub):
        a = u_ref[0, r * tm:(r + 1) * tm, :]
        conv = []
        for si, (w_ref, cw, cb) in enumerate(streams):
            h = jnp.dot(a, w_ref[...], preferred_element_type=F32)
            acc = cb[...] + h * cw[2:3, :]
            for k in (1, 2):
                rk = pltpu.roll(h, k, 0)
                hk = jnp.concatenate([jnp.where(sub < k, tails[si][k - 1], rk[:8]), rk[8:]], axis=0)
                tails[si][k - 1] = rk[:8]
                acc = acc + hk * cw[2 - k:3 - k, :]
            conv.append(acc)
        act_ref[0, r * tm:(r + 1) * tm, :] = (_silu(conv[0]) * conv[1]).astype(BF16)
    sg_ref[0] = tails[0][1][0:2]
    sv_ref[0] = tails[1][1][0:2]


def _up_sample_kernel(u_ref, pg_ref, pv_ref, wg_ref, wv_ref, cwg_ref, cwv_ref, cbg_ref, cbv_ref,
                      act_ref, sg_ref, sv_ref, *rest, nb, ds, emit):
    hs_g, hs_v = rest[-2:]
    a = u_ref[...]
    for si, (w_ref, p_ref, hs, s_ref) in enumerate(((wg_ref, pg_ref, hs_g, sg_ref), (wv_ref, pv_ref, hs_v, sv_ref))):
        w = w_ref[...]
        if emit:
            w = w.astype(BF16)
            rest[si][...] = w
        h = jnp.dot(a, w, preferred_element_type=F32)
        hs[:, 6:8, :] = p_ref[...]
        hs[:, 8:8 + ds, :] = h.reshape(nb, ds, h.shape[-1])
        s_ref[...] = hs[:, 6 + ds:8 + ds, :]
    act = _conv_gate(hs_g, hs_v, 8, ds, cwg_ref, cwv_ref, cbg_ref, cbv_ref, 1)
    act_ref[...] = act.reshape(nb * ds, act.shape[-1]).astype(BF16)


def _ffn_up_sample(h, w_gate, w_value, conv_w, conv_b, layer, state):
    b, s, d = h.shape
    emit = w_gate.ndim == 3
    f = w_gate.shape[-1] // 2 if emit else w_gate.shape[-1]
    tf = _pick(f, 256, 128)
    nf = f // tf
    cb3 = conv_b.reshape(conv_b.shape[0], 1, 2 * f)
    g2 = lambda j: (layer, 0, j)
    v2 = lambda j: (layer, 0, nf + j)
    col = lambda j: (0, j)
    st_spec = pl.BlockSpec((b, 2, tf), lambda j: (0, 0, j))
    w_specs = ([pl.BlockSpec((None, d, tf), g2), pl.BlockSpec((None, d, tf), v2)] if emit
               else [pl.BlockSpec((d, tf), col), pl.BlockSpec((d, tf), col)])
    copies = [jax.ShapeDtypeStruct((d, f), BF16)] * 2 if emit else []
    outs = pl.pallas_call(
        functools.partial(_up_sample_kernel, nb=b, ds=s, emit=emit),
        grid=(nf,),
        in_specs=[pl.BlockSpec((b * s, d), lambda j: (0, 0)),
                  pl.BlockSpec((None, b, 2, tf), lambda j: (layer, 0, 0, j)),
                  pl.BlockSpec((None, b, 2, tf), lambda j: (layer, 0, 0, nf + j))]
                 + w_specs
                 + [pl.BlockSpec((None, 3, tf), g2), pl.BlockSpec((None, 3, tf), v2),
                    pl.BlockSpec((None, 1, tf), g2), pl.BlockSpec((None, 1, tf), v2)],
        out_specs=[pl.BlockSpec((b * s, tf), col), st_spec, st_spec] + [pl.BlockSpec((d, tf), col)] * len(copies),
        out_shape=[jax.ShapeDtypeStruct((b * s, f), BF16),
                   jax.ShapeDtypeStruct((b, 2, f), F32), jax.ShapeDtypeStruct((b, 2, f), F32)] + copies,
        scratch_shapes=[pltpu.VMEM((b, 8 + s, tf), F32), pltpu.VMEM((b, 8 + s, tf), F32)],
        compiler_params=_cparams(("arbitrary",), 40),
        name="ffn_up_sample",
    )(h.reshape(b * s, d), state, state, w_gate, w_value, conv_w, conv_w, cb3, cb3)
    return (outs[0], jnp.concatenate(outs[1:3], axis=-1)) + tuple(outs[3:])


def _ffn_up_prompt(h, wgb, wvb, conv_w, conv_b, w_down, layer):
    b, s, d = h.shape
    f = wgb.shape[1]
    tf = min(f, 512)
    n_col = pl.cdiv(f, tf)
    n_sub = 8 if s % 128 == 0 else 1
    tr = -(-pl.cdiv(f, n_col) // 16) * 16
    tc = d // b
    assert tc % 128 == 0 and (n_col - 1) * tr < f <= n_col * tr
    col = lambda bb, j: (0, j)
    st_spec = pl.BlockSpec((1, 2, tf), lambda bb, j: (bb, 0, j))
    cw, cb = conv_w[layer], conv_b[layer].reshape(1, 2 * f)
    act, sg, sv, w_down_b = pl.pallas_call(
        functools.partial(_up_prompt_kernel, tm=s // n_sub, n_sub=n_sub),
        grid=(b, n_col),
        in_specs=[pl.BlockSpec((1, s, d), lambda bb, j: (bb, 0, 0), pipeline_mode=pl.Buffered(1)),
                  pl.BlockSpec((d, tf), col), pl.BlockSpec((d, tf), col),
                  pl.BlockSpec((3, tf), col), pl.BlockSpec((3, tf), col),
                  pl.BlockSpec((1, tf), col), pl.BlockSpec((1, tf), col),
                  pl.BlockSpec((None, tr, tc), lambda bb, j: (layer, j, bb))],
        out_specs=[pl.BlockSpec((1, s, tf), lambda bb, j: (bb, 0, j)), st_spec, st_spec,
                   pl.BlockSpec((tr, tc), lambda bb, j: (j, bb))],
        out_shape=[jax.ShapeDtypeStruct((b, s, f), BF16),
                   jax.ShapeDtypeStruct((b, 2, f), F32), jax.ShapeDtypeStruct((b, 2, f), F32),
                   jax.ShapeDtypeStruct((f, d), BF16)],
        compiler_params=_cparams(("arbitrary", "arbitrary"),
                                 (2 * s * d + 8 * d * tf + 40 * s * tf + 12 * tr * tc) // 2 ** 20 + 8),
        name="ffn_up_prompt",
    )(h, wgb, wvb, cw[:, :f], cw[:, f:], cb[:, :f], cb[:, f:], w_down)
    return act.reshape(b * s, f), jnp.concatenate([sg, sv], axis=-1), w_down_b


def kernel(x_prompt, x_sample, c_prompt, c_sample, cache_ckv, cache_kpe, state_pool, state_conv, w_ada, b_ada,
           norm_mix_w, norm_ffn_w, w_dq, q_norm_w, w_uq, w_dkv, kv_norm_w, w_ukv, w_o, w_pool, pool_scale,
           w_up, conv_w, conv_b, w_down, final_norm_w):
    bp, s, d = x_prompt.shape
    bs, ds, _ = x_sample.shape
    past = cache_ckv.shape[2]
    depth = w_ada.shape[0]
    n_heads = w_uq.shape[2]
    q_lora = w_dq.shape[2]
    kv_lora = w_dkv.shape[2] - 64
    qk_head = w_uq.shape[3]
    scale = float(qk_head) ** -0.5
    pos_p = jnp.arange(s, dtype=jnp.int32)
    pos_s = jnp.tile(past + jnp.arange(ds, dtype=jnp.int32), bs)

    nb_all = bp + bs
    pad = (-nb_all) % 16
    c_all = jnp.concatenate([c_prompt, c_sample, jnp.zeros((pad, d), F32)], axis=0)
    mod_all = _ada(c_all, w_ada, b_ada)

    xp, xs = x_prompt, x_sample
    ckv_p_l, kpe_p_l, ckv_s_l, kpe_s_l = [], [], [], []
    pool_p_l, pool_s_l, conv_p_l, conv_s_l = [], [], [], []
    for i in range(depth):
        j = i // 2
        mod_p = mod_all[i, :bp].reshape(bp, 6, 1, d)
        mod_s = mod_all[i, bp:nb_all].reshape(bs, 6, 1, d)
        w_up_b = None
        if i % 2 == 0:
            w_in = jnp.concatenate([w_dq[j], w_dkv[j], jnp.zeros((d, 64), F32)], axis=1).astype(BF16)
            w_q_nope = w_uq[j][:, :, :128].reshape(q_lora, n_heads * 128).astype(BF16)
            w_q_pe = w_uq[j][:, :, 128:].reshape(q_lora, n_heads * 64).astype(BF16)
            w_ukv3 = w_ukv.reshape(w_ukv.shape[0], kv_lora, n_heads * 256)
            cq, ckv_p, ckv_pb, kpe_p, kpe_pb = _mla_project(xp, norm_mix_w, i, mod_p, pos_p, w_in, q_norm_w, kv_norm_w, j,
                                                            q_lora, kv_lora)
            qn, qp = _q_heads(cq, w_q_nope, w_q_pe, pos_p, scale)
            kn, vv = _kv_heads(ckv_pb, w_ukv3, j, n_heads)
            r3 = lambda t: t.reshape(bp, s, t.shape[-1])
            o_p, *w_up_b = _flash(r3(qn), r3(qp), r3(kn), r3(vv), r3(kpe_pb), n_heads, w_up, i)
            xp = _proj_residual("attn_out_prompt", o_p.reshape(bp * s, n_heads * 128), w_o, j, xp, mod_p, 2)
            cq, ckv_s, _, kpe_s, _ = _mla_project(xs, norm_mix_w, i, mod_s, pos_s, w_in, q_norm_w, kv_norm_w, j,
                                                  q_lora, kv_lora)
            qn, qp = _q_heads(cq, w_q_nope, w_q_pe, pos_s, scale)
            q_lat = _absorb_q(qn, w_ukv3, j, n_heads, bs, ds)
            qp_heads = qp.reshape(bs, ds, n_heads, 64).transpose(0, 2, 1, 3)
            o_lat = _sample_attention(q_lat, qp_heads, cache_ckv, jnp.swapaxes(cache_kpe, 2, 3), j, ckv_s.reshape(bs, ds, kv_lora),
                                      kpe_s.reshape(bs, ds, 64))
            o_s = _expand_v(o_lat, w_ukv3, j)
            xs = _proj_residual("attn_out_sample", o_s, w_o, j, xs, mod_s, 2)
            ckv_p_l.append(ckv_p.reshape(bp, s, kv_lora))
            kpe_p_l.append(kpe_p.reshape(bp, s, 64))
            ckv_s_l.append(ckv_s.reshape(bs, ds, kv_lora))
            kpe_s_l.append(kpe_s.reshape(bs, ds, 64))
        else:
            hist = state_pool.shape[2]
            xp, tail_p = _pool_mix_start(xp, norm_mix_w, i, mod_p, w_pool, pool_scale, j)
            us = _normmod(xs, norm_mix_w, i, mod_s, 0, 1, F32)
            pool_p_l.append(tail_p[:, tail_p.shape[1] - hist:])
            pool_s_l.append(jnp.concatenate([state_pool[j], us], axis=1)[:, -hist:])
            xs = _pool_mix(us, state_pool, w_pool, pool_scale, j, xs, mod_s, 2, past)
        hp = _normmod(xp, norm_ffn_w, i, mod_p, 3, 4, BF16)
        hs = _normmod(xs, norm_ffn_w, i, mod_s, 3, 4, BF16)
        if w_up_b is None:
            act_s, cst_s, *w_up_b = _ffn_up_sample(hs, w_up, w_up, conv_w, conv_b, i, state_conv)
        else:
            act_s, cst_s = _ffn_up_sample(hs, *w_up_b, conv_w, conv_b, i, state_conv)
        act_p, cst_p, w_down_b = _ffn_up_prompt(hp, *w_up_b, conv_w, conv_b, w_down, i)
        conv_p_l.append(cst_p)
        conv_s_l.append(cst_s)
        xs = _proj_residual("ffn_down_sample", act_s, w_down_b, None, xs, mod_s, 5)
        xp = _proj_residual("ffn_down_prompt", act_p, w_down_b, None, xp, mod_p, 5)

    y_prompt = _final_norm(xp, final_norm_w)
    y_sample = _final_norm(xs, final_norm_w)
    return (y_prompt, y_sample,
            jnp.stack(ckv_p_l), jnp.stack(kpe_p_l), jnp.stack(ckv_s_l), jnp.stack(kpe_s_l),
            jnp.stack(pool_p_l), jnp.stack(pool_s_l), jnp.stack(conv_p_l), jnp.stack(conv_s_l))
```

```python
import functools

import jax
import jax.numpy as jnp
from jax import lax
from jax.experimental import pallas as pl
from jax.experimental.pallas import tpu as pltpu

F32 = jnp.float32
BF16 = jnp.bfloat16

CHUNK = 64
ROPE_THETA = 10000.0
RMS_EPS = 1e-6
NEG_INF = -1e30
POOL_WINDOWS = (2, 4, 8, 16)
POOL_HALO = 32
V7X_VMEM_CAP_MB = 56


def _cparams(sem, vmem_mb, flags=None):
    return pltpu.CompilerParams(dimension_semantics=sem, vmem_limit_bytes=min(vmem_mb, V7X_VMEM_CAP_MB) * 2 ** 20,
                                flags=flags)


def _pick(dim, pref, align):
    t = min(pref, dim)
    t -= t % align
    while t >= align:
        if dim % t == 0:
            return t
        t -= align
    return dim


def _silu(x):
    return x * (0.5 + 0.5 * jnp.tanh(0.5 * x))


def _rms(x, w):
    return x * lax.rsqrt(jnp.mean(x * x, axis=-1, keepdims=True) + RMS_EPS) * w


def _rope_rot(x):
    n = x.shape[-1]
    lane = lax.broadcasted_iota(jnp.int32, x.shape, x.ndim - 1)
    return jnp.where(lane % 64 < 32, -pltpu.roll(x, n - 32, x.ndim - 1), pltpu.roll(x, 32, x.ndim - 1))


def _rope(x, cos_ref, sin_ref):
    reps = x.shape[-1] // 128
    c = cos_ref[...]
    s = sin_ref[...]
    if reps > 1:
        c = jnp.concatenate([c] * reps, axis=-1)
        s = jnp.concatenate([s] * reps, axis=-1)
    return x * c + _rope_rot(x) * s


def _ada_kernel(c_ref, w_ref, b_ref, o_ref):
    a = _silu(c_ref[...]).astype(BF16)
    o_ref[...] = jnp.dot(a, w_ref[...].astype(BF16), preferred_element_type=F32) + b_ref[...]


def _ada(c, w_ada, b_ada):
    n_layers, d, n = w_ada.shape
    bc = c.shape[0]
    tn = _pick(n, 512, 128)
    return pl.pallas_call(
        _ada_kernel,
        grid=(n_layers, n // tn),
        in_specs=[pl.BlockSpec((bc, d), lambda l, j: (0, 0)),
                  pl.BlockSpec((None, d, tn), lambda l, j: (l, 0, j)),
                  pl.BlockSpec((None, 1, tn), lambda l, j: (l, 0, j))],
        out_specs=pl.BlockSpec((None, bc, tn), lambda l, j: (l, 0, j)),
        out_shape=jax.ShapeDtypeStruct((n_layers, bc, n), F32),
        compiler_params=_cparams(("parallel", "parallel"), 4 * d * tn * 3 // 2 ** 20 + 8),
        name="ada",
    )(c, w_ada, b_ada.reshape(n_layers, 1, n))


def _normmod_kernel(x_ref, w_ref, sh_ref, sc_ref, o_ref):
    y = _rms(x_ref[...], w_ref[...])
    o_ref[...] = (y * (1.0 + sc_ref[...]) + sh_ref[...]).astype(o_ref.dtype)


def _norm_kernel(x_ref, w_ref, o_ref):
    o_ref[...] = _rms(x_ref[...], w_ref[...]).astype(o_ref.dtype)


def _row_blocks(b, s):
    if s >= 512:
        return 1, _pick(s, 512, 16)
    return _pick(b, max(1, 512 // s), 1), s


def _norm_vmem_mb(block_elems, out_dtype):
    return block_elems * (2 * 4 + 2 * jnp.dtype(out_dtype).itemsize + 2 * 4) // 2 ** 20 + 4


def _normmod(x, w, layer, mod, k_shift, k_scale, out_dtype):
    b, s, d = x.shape
    nb, ts = _row_blocks(b, s)
    w3 = w.reshape(w.shape[0], 1, d)
    return pl.pallas_call(
        _normmod_kernel,
        grid=(b // nb, s // ts),
        in_specs=[pl.BlockSpec((nb, ts, d), lambda i, j: (i, j, 0)),
                  pl.BlockSpec((1, 1, d), lambda i, j: (layer, 0, 0)),
                  pl.BlockSpec((nb, None, 1, d), lambda i, j: (i, k_shift, 0, 0)),
                  pl.BlockSpec((nb, None, 1, d), lambda i, j: (i, k_scale, 0, 0))],
        out_specs=pl.BlockSpec((nb, ts, d), lambda i, j: (i, j, 0)),
        out_shape=jax.ShapeDtypeStruct((b, s, d), out_dtype),
        compiler_params=_cparams(("parallel", "parallel"), _norm_vmem_mb(nb * ts * d, out_dtype)),
        name="normmod",
    )(x, w3, mod, mod)


def _final_norm(x, w):
    b, s, d = x.shape
    nb, ts = _row_blocks(b, s)
    return pl.pallas_call(
        _norm_kernel,
        grid=(b // nb, s // ts),
        in_specs=[pl.BlockSpec((nb, ts, d), lambda i, j: (i, j, 0)),
                  pl.BlockSpec((1, 1, d), lambda i, j: (0, 0, 0))],
        out_specs=pl.BlockSpec((nb, ts, d), lambda i, j: (i, j, 0)),
        out_shape=jax.ShapeDtypeStruct((b, s, d), F32),
        compiler_params=_cparams(("parallel", "parallel"), _norm_vmem_mb(nb * ts * d, F32)),
        name="final_norm",
    )(x, w.reshape(1, 1, d))


def _mm(name, grid, a, a_spec, b, b_spec, extras, outs, epilogue, *, prologue=None, trans_b=False, sem, vmem_mb):
    n_ex, n_out = len(extras), len(outs)
    dn = (((1,), (1 if trans_b else 0,)), ((), ()))

    def kern(*refs):
        ex = refs[2:2 + n_ex]
        a_blk = refs[0][...]
        if prologue is not None:
            a_blk = prologue(a_blk, ex)
        a_blk = a_blk.reshape(-1, a_blk.shape[-1]).astype(BF16)
        acc = lax.dot_general(a_blk, refs[1][...].astype(BF16), dn, preferred_element_type=F32)
        epilogue(acc, ex, refs[2 + n_ex:2 + n_ex + n_out])

    return pl.pallas_call(
        kern,
        grid=grid,
        in_specs=[a_spec, b_spec] + [s for _, s in extras],
        out_specs=[s for _, s in outs],
        out_shape=[sd for sd, _ in outs],
        compiler_params=_cparams(sem, vmem_mb),
        name=name,
    )(a, b, *[x for x, _ in extras])


def _rope_tables(pos):
    half = 32
    inv = jnp.power(ROPE_THETA, -jnp.arange(half, dtype=F32) / half)
    ang = pos.astype(F32)[:, None] * inv[None, :]
    return jnp.tile(jnp.cos(ang), (1, 4)), jnp.tile(jnp.sin(ang), (1, 4))


def _mla_project(x, norm_w, norm_layer, mod, pos_rows, w_in, q_norm_w, kv_norm_w, layer, q_lora, kv_lora):
    b, s, d = x.shape
    m = b * s
    n = w_in.shape[1]
    period = pos_rows.shape[0]
    tm = _pick(period, 512, 16)
    cos, sin = _rope_tables(pos_rows)
    npb = period // tm
    if s % tm == 0:
        per = s // tm
        mods = [(mod, pl.BlockSpec((None, None, 1, d), lambda i, k=k: (i // per, k, 0, 0))) for k in (0, 1)]
    else:
        mods = [(jnp.repeat(mod[:, k, 0, :], s, axis=0), pl.BlockSpec((tm, d), lambda i: (i, 0))) for k in (0, 1)]

    def prologue(a, ex):
        return _rms(a, ex[0][...]) * (1.0 + ex[2][...]) + ex[1][...]

    def epilogue(acc, ex, o):
        qw_ref, kw_ref, cos_ref, sin_ref = ex[3:]
        o[0][...] = _rms(acc[:, :q_lora], qw_ref[...]).astype(BF16)
        ckv = _rms(acc[:, q_lora:q_lora + kv_lora], kw_ref[...])
        o[1][...] = ckv
        o[2][...] = ckv.astype(BF16)
        kpe = _rope(acc[:, q_lora + kv_lora:], cos_ref, sin_ref)[:, :64]
        o[3][...] = kpe
        o[4][...] = kpe.astype(BF16)

    row = lambda width: pl.BlockSpec((tm, width), lambda i: (i, 0))
    return _mm(
        "mla_in", (m // tm,), x.reshape(m, d), pl.BlockSpec((tm, d), lambda i: (i, 0)),
        w_in, pl.BlockSpec((d, n), lambda i: (0, 0), pipeline_mode=pl.Buffered(1)),
        [(norm_w.reshape(-1, 1, d), pl.BlockSpec((None, 1, d), lambda i: (norm_layer, 0, 0)))] + mods +
        [(q_norm_w.reshape(-1, 1, q_lora), pl.BlockSpec((None, 1, q_lora), lambda i: (layer, 0, 0))),
         (kv_norm_w.reshape(-1, 1, kv_lora), pl.BlockSpec((None, 1, kv_lora), lambda i: (layer, 0, 0))),
         (cos, pl.BlockSpec((tm, 128), lambda i: (i % npb, 0))),
         (sin, pl.BlockSpec((tm, 128), lambda i: (i % npb, 0)))],
        [(jax.ShapeDtypeStruct((m, q_lora), BF16), row(q_lora)),
         (jax.ShapeDtypeStruct((m, kv_lora), F32), row(kv_lora)),
         (jax.ShapeDtypeStruct((m, kv_lora), BF16), row(kv_lora)),
         (jax.ShapeDtypeStruct((m, 64), F32), row(64)),
         (jax.ShapeDtypeStruct((m, 64), BF16), row(64))],
        epilogue, prologue=prologue, sem=("parallel",),
        vmem_mb=(2 * d * n + (8 + 12 + (0 if s % tm == 0 else 16)) * tm * d + 16 * tm * n) // 2 ** 20 + 8)


def _q_heads(cq, w_nope, w_pe, pos_rows, scale):
    m, kq = cq.shape
    period = pos_rows.shape[0]
    tm = _pick(period, 512, 16)
    npb = period // tm
    cos, sin = _rope_tables(pos_rows)

    def plain(acc, ex, o):
        o[0][...] = (acc * scale).astype(BF16)

    def roped(acc, ex, o):
        o[0][...] = (_rope(acc, ex[0], ex[1]) * scale).astype(BF16)

    outs = []
    for name, w, extras, epi in (("q_nope", w_nope, [], plain),
                                 ("q_pe", w_pe, [(cos, pl.BlockSpec((tm, 128), lambda i, j: (i % npb, 0))),
                                                 (sin, pl.BlockSpec((tm, 128), lambda i, j: (i % npb, 0)))], roped)):
        n = w.shape[1]
        tn = _pick(n, 2048, 128)
        outs.append(_mm(name, (m // tm, n // tn), cq, pl.BlockSpec((tm, kq), lambda i, j: (i, 0)),
                        w, pl.BlockSpec((kq, tn), lambda i, j: (0, j)), extras,
                        [(jax.ShapeDtypeStruct((m, n), BF16), pl.BlockSpec((tm, tn), lambda i, j: (i, j)))],
                        epi, sem=("parallel", "parallel"), vmem_mb=40)[0])
    return outs


def _kv_heads(ckv_b, w_ukv3, layer, n_heads):
    m, kc = ckv_b.shape
    tm = _pick(m, 512, 16)
    hg = 4 if n_heads % 4 == 0 else 1

    def kern(c_ref, w_ref, k_ref, v_ref):
        c = c_ref[...]
        for g in range(n_heads // hg):
            acc = jnp.dot(c, w_ref[:, g * hg * 256:(g + 1) * hg * 256].astype(BF16), preferred_element_type=F32)
            for h in range(hg):
                cols = slice((g * hg + h) * 128, (g * hg + h + 1) * 128)
                k_ref[:, cols] = acc[:, h * 256:h * 256 + 128].astype(BF16)
                v_ref[:, cols] = acc[:, h * 256 + 128:(h + 1) * 256].astype(BF16)

    out = jax.ShapeDtypeStruct((m, n_heads * 128), BF16)
    out_spec = pl.BlockSpec((tm, n_heads * 128), lambda i: (i, 0))
    return pl.pallas_call(
        kern,
        grid=(m // tm,),
        in_specs=[pl.BlockSpec((tm, kc), lambda i: (i, 0)),
                  pl.BlockSpec((None, kc, n_heads * 256), lambda i: (layer, 0, 0), pipeline_mode=pl.Buffered(1))],
        out_specs=[out_spec, out_spec],
        out_shape=[out, out],
        compiler_params=_cparams(("parallel",), (4 * kc * n_heads * 256 + 8 * tm * n_heads * 128) // 2 ** 20 + 16),
        name="kv_heads",
    )(ckv_b, w_ukv3)


def _gate_specs(x, mod, k_gate, tm, tn):
    b, s, d = x.shape
    x2 = x.reshape(b * s, d)
    x_spec = pl.BlockSpec((tm, tn), lambda i, j: (i, j))
    if s % tm == 0:
        per = s // tm
        g = (mod, pl.BlockSpec((None, None, 1, tn), lambda i, j: (i // per, k_gate, 0, j)))
    else:
        rows = jnp.repeat(mod[:, k_gate, 0, :], s, axis=0)
        g = (rows, pl.BlockSpec((tm, tn), lambda i, j: (i, j)))
    return [(x2, x_spec), g]


LHS_TILE_BYTES = 12 * 2 ** 20


def _proj_residual(name, a2, w, layer, x, mod, k_gate):
    m, kk = a2.shape
    n = w.shape[-1]
    b, s, _ = x.shape
    rows = min(1024, LHS_TILE_BYTES // (2 * kk))
    tm = _pick(s, rows, 16) if s >= 256 else _pick(m, rows, 16)
    tn = _pick(n, 512, 128)
    w_spec = (pl.BlockSpec((None, kk, tn), lambda i, j: (layer, 0, j)) if w.ndim == 3
              else pl.BlockSpec((kk, tn), lambda i, j: (0, j)))

    def epilogue(acc, ex, o):
        o[0][...] = ex[0][...] + ex[1][...] * acc

    out = _mm(name, (m // tm, n // tn), a2, pl.BlockSpec((tm, kk), lambda i, j: (i, 0)),
              w, w_spec, _gate_specs(x, mod, k_gate, tm, tn),
              [(jax.ShapeDtypeStruct((m, n), F32), pl.BlockSpec((tm, tn), lambda i, j: (i, j)))],
              epilogue, sem=("parallel", "arbitrary"),
              vmem_mb=(4 * tm * kk + (10 if w.ndim == 3 else 4) * kk * tn + 36 * tm * tn) // 2 ** 20 + 8)[0]
    return out.reshape(b, s, n)


def _flash_kernel(qn_ref, qp_ref, kn_ref, v_ref, kpe_ref, wg_ref, wv_ref, o_ref, wgb_ref, wvb_ref, *, t, nq, n_cast):
    @pl.when(pl.program_id(0) * pl.num_programs(1) + pl.program_id(1) < n_cast)
    def _():
        wgb_ref[...] = wg_ref[...].astype(BF16)
        wvb_ref[...] = wv_ref[...].astype(BF16)

    nt = (((1,), (1,)), ((), ()))
    kpe = kpe_ref[0]
    qc = lax.broadcasted_iota(jnp.int32, (t, t), 0) // CHUNK
    kc = lax.broadcasted_iota(jnp.int32, (t, t), 1) // CHUNK
    diag_visible = kc <= qc
    for hh in range(2):
        hs = slice(hh * 128, (hh + 1) * 128)
        kcat = jnp.concatenate([kn_ref[0, :, hs], kpe], axis=-1)
        for qi in range(nq):
            lo = qi * t
            rows = slice(lo, lo + t)
            q = jnp.concatenate([qn_ref[0, rows, hs], qp_ref[0, rows, hh * 64:(hh + 1) * 64]], axis=-1)
            s_d = jnp.where(diag_visible, lax.dot_general(q, kcat[lo:lo + t], nt, preferred_element_type=F32), NEG_INF)
            m = jnp.max(s_d, axis=-1, keepdims=True)
            if qi > 0:
                s_o = lax.dot_general(q, kcat[:lo], nt, preferred_element_type=F32)
                m = jnp.maximum(m, jnp.max(s_o, axis=-1, keepdims=True))
                p_o = jnp.exp(s_o - m)
            p_d = jnp.exp(s_d - m)
            l = jnp.sum(p_d, axis=-1, keepdims=True)
            acc = jnp.dot(p_d.astype(BF16), v_ref[0, rows, hs], preferred_element_type=F32)
            if qi > 0:
                l = l + jnp.sum(p_o, axis=-1, keepdims=True)
                acc = acc + jnp.dot(p_o.astype(BF16), v_ref[0, 0:lo, hs], preferred_element_type=F32)
            o_ref[0, rows, hs] = (acc / l).astype(BF16)


def _flash(qn, qp, kn, v, kpe, n_heads, w_up, layer):
    b, s, _ = qn.shape
    d, f = w_up.shape[1], w_up.shape[2] // 2
    t = _pick(s, 512, CHUNK)
    tf = _pick(f, 256, 128)
    n_cast = f // tf
    hp_steps = n_heads // 2
    assert n_cast <= b * hp_steps
    cast_tile = lambda bb, hp: jnp.minimum(bb * hp_steps + hp, n_cast - 1)
    kern = functools.partial(_flash_kernel, t=t, nq=s // t, n_cast=n_cast)
    wide = pl.BlockSpec((1, s, 256), lambda bb, hp: (bb, 0, hp))
    wb_spec = pl.BlockSpec((d, tf), lambda bb, hp: (0, cast_tile(bb, hp)))
    return pl.pallas_call(
        kern,
        grid=(b, hp_steps),
        in_specs=[wide, pl.BlockSpec((1, s, 128), lambda bb, hp: (bb, 0, hp)), wide, wide,
                  pl.BlockSpec((1, s, 64), lambda bb, hp: (bb, 0, 0)),
                  pl.BlockSpec((None, d, tf), lambda bb, hp: (layer, 0, cast_tile(bb, hp))),
                  pl.BlockSpec((None, d, tf), lambda bb, hp: (layer, 0, n_cast + cast_tile(bb, hp)))],
        out_specs=[wide, wb_spec, wb_spec],
        out_shape=[jax.ShapeDtypeStruct((b, s, n_heads * 128), BF16),
                   jax.ShapeDtypeStruct((d, f), BF16), jax.ShapeDtypeStruct((d, f), BF16)],
        compiler_params=_cparams(("arbitrary", "arbitrary"), 30 + 24 * d * tf // 2 ** 20),
        name="flash_prompt",
    )(qn, qp, kn, v, kpe, w_up, w_up)


def _sattn_kernel(ql_ref, qp_ref, ck_ref, kp_ref, nck_ref, nkp_ref, o_ref, *, past, n_split):
    nt = (((1,), (1,)), ((), ()))
    n_heads, ds, c = ql_ref.shape[1:]
    ck = ck_ref[0].astype(BF16)
    kp = kp_ref[0].astype(BF16)
    nck = nck_ref[0].astype(BF16)
    nkp = nkp_ref[0].astype(BF16)
    hg = n_heads // n_split
    rows = hg * ds
    qpos = past + lax.broadcasted_iota(jnp.int32, (rows, ds), 0) % ds
    kpos = past + lax.broadcasted_iota(jnp.int32, (rows, ds), 1)
    new_visible = kpos // CHUNK <= qpos // CHUNK
    for g in range(n_split):
        ql = ql_ref[0, g * hg:(g + 1) * hg].reshape(rows, c)
        qp = qp_ref[0, g * hg:(g + 1) * hg].reshape(rows, qp_ref.shape[-1])
        s_c = lax.dot_general(ql, ck, nt, preferred_element_type=F32) + jnp.dot(qp, kp, preferred_element_type=F32)
        s_n = lax.dot_general(ql, nck, nt, preferred_element_type=F32) + lax.dot_general(qp, nkp, nt, preferred_element_type=F32)
        s_n = jnp.where(new_visible, s_n, NEG_INF)
        m = jnp.maximum(jnp.max(s_c, axis=-1, keepdims=True), jnp.max(s_n, axis=-1, keepdims=True))
        p_c = jnp.exp(s_c - m)
        p_n = jnp.exp(s_n - m)
        l = jnp.sum(p_c, axis=-1, keepdims=True) + jnp.sum(p_n, axis=-1, keepdims=True)
        acc = (jnp.dot(p_c.astype(BF16), ck, preferred_element_type=F32)
               + jnp.dot(p_n.astype(BF16), nck, preferred_element_type=F32))
        o_ref[0, g * hg:(g + 1) * hg] = (acc / l).astype(BF16).reshape(hg, ds, c)


def _sample_attention(q_lat, q_pe, cache_ckv, cache_kpe_t, layer, ckv_new, kpe_new):
    b, n_heads, ds, c = q_lat.shape
    past = cache_ckv.shape[2]
    n_split = 2 if n_heads % 2 == 0 else 1
    kern = functools.partial(_sattn_kernel, past=past, n_split=n_split)
    per_stream = lambda *blk: pl.BlockSpec((1,) + blk, lambda bb: (bb,) + (0,) * len(blk))
    rows = n_heads * ds // n_split
    return pl.pallas_call(
        kern,
        grid=(b,),
        in_specs=[per_stream(n_heads, ds, c), per_stream(n_heads, ds, 64),
                  pl.BlockSpec((None, 1, past, c), lambda bb: (layer, bb, 0, 0)),
                  pl.BlockSpec((None, 1, 64, past), lambda bb: (layer, bb, 0, 0)),
                  per_stream(ds, c), per_stream(ds, 64)],
        out_specs=per_stream(n_heads, ds, c),
        out_shape=jax.ShapeDtypeStruct((b, n_heads, ds, c), BF16),
        compiler_params=_cparams(("parallel",), (2 * 4 * past * (c + 128) + 2 * past * c + 12 * rows * past) // 2 ** 20 + 10),
        name="attn_sample",
    )(q_lat, q_pe, cache_ckv, cache_kpe_t, ckv_new, kpe_new)


def _absorb_q(qn, w_ukv3, layer, n_heads, b, ds):
    m = qn.shape[0]
    c = w_ukv3.shape[1]

    def epilogue(acc, ex, o):
        o[0][...] = acc.astype(BF16).reshape(b, ds, c)

    return _mm("absorb_q", (n_heads,), qn, pl.BlockSpec((m, 128), lambda h: (0, h)),
               w_ukv3, pl.BlockSpec((None, c, 128), lambda h: (layer, 0, 2 * h)), [],
               [(jax.ShapeDtypeStruct((b, n_heads, ds, c), BF16), pl.BlockSpec((b, None, ds, c), lambda h: (0, h, 0, 0)))],
               epilogue, trans_b=True, sem=("parallel",), vmem_mb=16)[0]


def _expand_v(o_lat, w_ukv3, layer):
    b, n_heads, ds, c = o_lat.shape
    m = b * ds

    def epilogue(acc, ex, o):
        o[0][...] = acc.astype(BF16)

    return _mm("expand_v", (n_heads,), o_lat, pl.BlockSpec((b, None, ds, c), lambda h: (0, h, 0, 0)),
               w_ukv3, pl.BlockSpec((None, c, 128), lambda h: (layer, 0, 2 * h + 1)), [],
               [(jax.ShapeDtypeStruct((m, n_heads * 128), BF16), pl.BlockSpec((m, 128), lambda h: (0, h)))],
               epilogue, sem=("parallel",), vmem_mb=16)[0]


def _pool_kernel(u_ref, prev_ref, w_ref, ps_ref, x_ref, g_ref, o_ref, st, *, pos0):
    grp = pl.program_id(0)
    nb, ts, gw = u_ref.shape
    prev_rows = prev_ref.shape[1]
    total = POOL_HALO + ts
    u = u_ref[...]
    st[:, 0:POOL_HALO, :] = jnp.zeros((nb, POOL_HALO, gw), F32)
    st[:, POOL_HALO - prev_rows:POOL_HALO, :] = prev_ref[...]
    st[:, POOL_HALO:total, :] = u
    pos = pos0 + lax.broadcasted_iota(jnp.int32, (nb, ts, gw), 1)

    def mix(k):
        s = st[...].reshape(nb * total, gw)
        for step in range(k + 1):
            s = s + pltpu.roll(s, 2 ** step, 0)
        win = s.reshape(nb, total, gw)[:, POOL_HALO:, :]
        cnt = jnp.minimum(pos + 1, POOL_WINDOWS[k]).astype(F32)
        d = (win / cnt - u).astype(BF16).reshape(nb * ts, gw)
        y = jnp.dot(d, w_ref[...].astype(BF16), preferred_element_type=F32).reshape(nb, ts, gw)
        o_ref[...] = x_ref[...] + g_ref[...] * (y * ps_ref[...])

    for k in range(len(POOL_WINDOWS)):
        pl.when(grp == k)(functools.partial(mix, k))


def _pool_mix(u, prev, w_pool, pool_scale, layer_j, x, mod, k_gate, pos0):
    b, s, d = u.shape
    ng = w_pool.shape[1]
    gw = d // ng
    nb = _pick(b, 8, 1)
    prev_rows = prev.shape[2]
    assert prev_rows <= POOL_HALO and s % 8 == 0
    blk = pl.BlockSpec((nb, s, gw), lambda g, i: (i, 0, g))
    return pl.pallas_call(
        functools.partial(_pool_kernel, pos0=pos0),
        grid=(ng, b // nb),
        in_specs=[blk,
                  pl.BlockSpec((None, nb, prev_rows, gw), lambda g, i: (layer_j, i, 0, g)),
                  pl.BlockSpec((None, None, gw, gw), lambda g, i: (layer_j, g, 0, 0)),
                  pl.BlockSpec((None, 1, 1, gw), lambda g, i: (layer_j, 0, 0, g)),
                  blk,
                  pl.BlockSpec((nb, None, 1, gw), lambda g, i: (i, k_gate, 0, g))],
        out_specs=blk,
        out_shape=jax.ShapeDtypeStruct((b, s, d), F32),
        scratch_shapes=[pltpu.VMEM((nb, POOL_HALO + s, gw), F32)],
        compiler_params=_cparams(("arbitrary", "arbitrary"), 32),
        name="pool_mix",
    )(u, prev, w_pool, pool_scale.reshape(pool_scale.shape[0], 1, 1, d), x, mod)


def _pool_start_kernel(x_ref, xh_ref, nw_ref, sh_ref, sc_ref, w_ref, ps_ref, g_ref, o_ref, tail_ref, wb_ref, *, ts):
    step = pl.program_id(1)

    @pl.when((pl.program_id(0) == 0) & (step == 0))
    def _():
        wb_ref[...] = w_ref[...].astype(BF16)

    def mixer_input(v):
        return _rms(v, nw_ref[...]) * (1.0 + sc_ref[...]) + sh_ref[...]

    x = x_ref[0]
    u = mixer_input(x)
    halo = mixer_input(xh_ref[0])
    ext = jnp.concatenate([jnp.where(step == 0, jnp.zeros_like(halo), halo), u], axis=0)
    tail_ref[0] = u[ts - tail_ref.shape[1]:, :]
    gw = w_ref.shape[-1]
    pos = step * ts + lax.broadcasted_iota(jnp.int32, (ts, gw), 0)
    for k, window in enumerate(POOL_WINDOWS):
        cols = slice(k * gw, (k + 1) * gw)
        s = ext[:, cols]
        for stage in range(k + 1):
            s = s + pltpu.roll(s, 2 ** stage, 0)
        cnt = jnp.minimum(pos + 1, window).astype(F32)
        d = (s[POOL_HALO:, :] / cnt - u[:, cols]).astype(BF16)
        y = jnp.dot(d, wb_ref[k], preferred_element_type=F32)
        o_ref[0, :, cols] = x[:, cols] + g_ref[:, cols] * (y * ps_ref[:, cols])


def _pool_mix_start(x, norm_w, norm_layer, mod, w_pool, pool_scale, layer_j):
    b, s, d = x.shape
    ng, gw = w_pool.shape[1], w_pool.shape[2]
    ts = _pick(s, 256, POOL_HALO)
    per = ts // POOL_HALO
    tail = 16
    mod_row = lambda k: pl.BlockSpec((None, None, 1, d), lambda i, j: (i, k, 0, 0))
    tile = pl.BlockSpec((1, ts, d), lambda i, j: (i, j, 0))
    return pl.pallas_call(
        functools.partial(_pool_start_kernel, ts=ts),
        grid=(b, s // ts),
        in_specs=[tile,
                  pl.BlockSpec((1, POOL_HALO, d), lambda i, j: (i, jnp.maximum(j * per - 1, 0), 0)),
                  pl.BlockSpec((None, 1, d), lambda i, j: (norm_layer, 0, 0)),
                  mod_row(0), mod_row(1),
                  pl.BlockSpec((None, ng, gw, gw), lambda i, j: (layer_j, 0, 0, 0), pipeline_mode=pl.Buffered(1)),
                  pl.BlockSpec((None, 1, d), lambda i, j: (layer_j, 0, 0)),
                  mod_row(2)],
        out_specs=[tile, pl.BlockSpec((1, tail, d), lambda i, j: (i, 0, 0))],
        out_shape=[jax.ShapeDtypeStruct((b, s, d), F32), jax.ShapeDtypeStruct((b, tail, d), F32)],
        scratch_shapes=[pltpu.VMEM((ng, gw, gw), BF16)],
        compiler_params=_cparams(("arbitrary", "arbitrary"), (6 * ng * gw * gw + 40 * ts * d) // 2 ** 20 + 8),
        name="pool_mix_start",
    )(x, x, norm_w.reshape(-1, 1, d), mod, mod, w_pool, pool_scale.reshape(-1, 1, d), mod)


def _conv_gate(hs_g, hs_v, base, rows, cw_g, cw_v, cb_g, cb_v, seq_axis):
    def conv(hs, cw, cb):
        def sl(off):
            idx = [slice(None)] * len(hs.shape)
            idx[seq_axis] = slice(base - off, base - off + rows)
            return hs[tuple(idx)]
        return cb[...] + sl(2) * cw[0:1, :] + sl(1) * cw[1:2, :] + sl(0) * cw[2:3, :]
    return _silu(conv(hs_g, cw_g, cb_g)) * conv(hs_v, cw_v, cb_v)


def _up_prompt_kernel(u_ref, wg_ref, wv_ref, cwg_ref, cwv_ref, cbg_ref, cbv_ref, wd_ref,
                      act_ref, sg_ref, sv_ref, wdb_ref, *, tm, n_sub):
    wdb_ref[...] = wd_ref[...].astype(BF16)
    tf = wg_ref.shape[1]
    streams = ((wg_ref, cwg_ref, cbg_ref), (wv_ref, cwv_ref, cbv_ref))
    sub = lax.broadcasted_iota(jnp.int32, (8, tf), 0)
    tails = [[jnp.zeros((8, tf), F32)] * 2 for _ in streams]
    for r in range(n_sub):
        a = u_ref[0, r * tm:(r + 1) * tm, :]
        conv = []
        for si, (w_ref, cw, cb) in enumerate(streams):
            h = jnp.dot(a, w_ref[...], preferred_element_type=F32)
            acc = cb[...] + h * cw[2:3, :]
            for k in (1, 2):
                rk = pltpu.roll(h, k, 0)
                hk = jnp.concatenate([jnp.where(sub < k, tails[si][k - 1], rk[:8]), rk[8:]], axis=0)
                tails[si][k - 1] = rk[:8]
                acc = acc + hk * cw[2 - k:3 - k, :]
            conv.append(acc)
        act_ref[0, r * tm:(r + 1) * tm, :] = (_silu(conv[0]) * conv[1]).astype(BF16)
    sg_ref[0] = tails[0][1][0:2]
    sv_ref[0] = tails[1][1][0:2]


def _up_sample_kernel(u_ref, pg_ref, pv_ref, wg_ref, wv_ref, cwg_ref, cwv_ref, cbg_ref, cbv_ref,
                      act_ref, sg_ref, sv_ref, *rest, nb, ds, emit):
    hs_g, hs_v = rest[-2:]
    a = u_ref[...]
    for si, (w_ref, p_ref, hs, s_ref) in enumerate(((wg_ref, pg_ref, hs_g, sg_ref), (wv_ref, pv_ref, hs_v, sv_ref))):
        w = w_ref[...]
        if emit:
            w = w.astype(BF16)
            rest[si][...] = w
        h = jnp.dot(a, w, preferred_element_type=F32)
        hs[:, 6:8, :] = p_ref[...]
        hs[:, 8:8 + ds, :] = h.reshape(nb, ds, h.shape[-1])
        s_ref[...] = hs[:, 6 + ds:8 + ds, :]
    act = _conv_gate(hs_g, hs_v, 8, ds, cwg_ref, cwv_ref, cbg_ref, cbv_ref, 1)
    act_ref[...] = act.reshape(nb * ds, act.shape[-1]).astype(BF16)


def _ffn_up_sample(h, w_gate, w_value, conv_w, conv_b, layer, state):
    b, s, d = h.shape
    emit = w_gate.ndim == 3
    f = w_gate.shape[-1] // 2 if emit else w_gate.shape[-1]
    tf = _pick(f, 256, 128)
    nf = f // tf
    cb3 = conv_b.reshape(conv_b.shape[0], 1, 2 * f)
    g2 = lambda j: (layer, 0, j)
    v2 = lambda j: (layer, 0, nf + j)
    col = lambda j: (0, j)
    st_spec = pl.BlockSpec((b, 2, tf), lambda j: (0, 0, j))
    w_specs = ([pl.BlockSpec((None, d, tf), g2), pl.BlockSpec((None, d, tf), v2)] if emit
               else [pl.BlockSpec((d, tf), col), pl.BlockSpec((d, tf), col)])
    copies = [jax.ShapeDtypeStruct((d, f), BF16)] * 2 if emit else []
    outs = pl.pallas_call(
        functools.partial(_up_sample_kernel, nb=b, ds=s, emit=emit),
        grid=(nf,),
        in_specs=[pl.BlockSpec((b * s, d), lambda j: (0, 0)),
                  pl.BlockSpec((None, b, 2, tf), lambda j: (layer, 0, 0, j)),
                  pl.BlockSpec((None, b, 2, tf), lambda j: (layer, 0, 0, nf + j))]
                 + w_specs
                 + [pl.BlockSpec((None, 3, tf), g2), pl.BlockSpec((None, 3, tf), v2),
                    pl.BlockSpec((None, 1, tf), g2), pl.BlockSpec((None, 1, tf), v2)],
        out_specs=[pl.BlockSpec((b * s, tf), col), st_spec, st_spec] + [pl.BlockSpec((d, tf), col)] * len(copies),
        out_shape=[jax.ShapeDtypeStruct((b * s, f), BF16),
                   jax.ShapeDtypeStruct((b, 2, f), F32), jax.ShapeDtypeStruct((b, 2, f), F32)] + copies,
        scratch_shapes=[pltpu.VMEM((b, 8 + s, tf), F32), pltpu.VMEM((b, 8 + s, tf), F32)],
        compiler_params=_cparams(("arbitrary",), 40),
        name="ffn_up_sample",
    )(h.reshape(b * s, d), state, state, w_gate, w_value, conv_w, conv_w, cb3, cb3)
    return (outs[0], jnp.concatenate(outs[1:3], axis=-1)) + tuple(outs[3:])


def _ffn_up_prompt(h, wgb, wvb, conv_w, conv_b, w_down, layer):
    b, s, d = h.shape
    f = wgb.shape[1]
    tf = min(f, 512)
    n_col = pl.cdiv(f, tf)
    n_sub = 8 if s % 128 == 0 else 1
    tr = -(-pl.cdiv(f, n_col) // 16) * 16
    tc = d // b
    assert tc % 128 == 0 and (n_col - 1) * tr < f <= n_col * tr
    col = lambda bb, j: (0, j)
    st_spec = pl.BlockSpec((1, 2, tf), lambda bb, j: (bb, 0, j))
    cw, cb = conv_w[layer], conv_b[layer].reshape(1, 2 * f)
    act, sg, sv, w_down_b = pl.pallas_call(
        functools.partial(_up_prompt_kernel, tm=s // n_sub, n_sub=n_sub),
        grid=(b, n_col),
        in_specs=[pl.BlockSpec((1, s, d), lambda bb, j: (bb, 0, 0), pipeline_mode=pl.Buffered(1)),
                  pl.BlockSpec((d, tf), col), pl.BlockSpec((d, tf), col),
                  pl.BlockSpec((3, tf), col), pl.BlockSpec((3, tf), col),
                  pl.BlockSpec((1, tf), col), pl.BlockSpec((1, tf), col),
                  pl.BlockSpec((None, tr, tc), lambda bb, j: (layer, j, bb))],
        out_specs=[pl.BlockSpec((1, s, tf), lambda bb, j: (bb, 0, j)), st_spec, st_spec,
                   pl.BlockSpec((tr, tc), lambda bb, j: (j, bb))],
        out_shape=[jax.ShapeDtypeStruct((b, s, f), BF16),
                   jax.ShapeDtypeStruct((b, 2, f), F32), jax.ShapeDtypeStruct((b, 2, f), F32),
                   jax.ShapeDtypeStruct((f, d), BF16)],
        compiler_params=_cparams(("arbitrary", "arbitrary"),
                                 (2 * s * d + 8 * d * tf + 40 * s * tf + 12 * tr * tc) // 2 ** 20 + 8),
        name="ffn_up_prompt",
    )(h, wgb, wvb, cw[:, :f], cw[:, f:], cb[:, :f], cb[:, f:], w_down)
    return act.reshape(b * s, f), jnp.concatenate([sg, sv], axis=-1), w_down_b


def kernel(x_prompt, x_sample, c_prompt, c_sample, cache_ckv, cache_kpe, state_pool, state_conv, w_ada, b_ada,
           norm_mix_w, norm_ffn_w, w_dq, q_norm_w, w_uq, w_dkv, kv_norm_w, w_ukv, w_o, w_pool, pool_scale,
           w_up, conv_w, conv_b, w_down, final_norm_w):
    bp, s, d = x_prompt.shape
    bs, ds, _ = x_sample.shape
    past = cache_ckv.shape[2]
    depth = w_ada.shape[0]
    n_heads = w_uq.shape[2]
    q_lora = w_dq.shape[2]
    kv_lora = w_dkv.shape[2] - 64
    qk_head = w_uq.shape[3]
    scale = float(qk_head) ** -0.5
    pos_p = jnp.arange(s, dtype=jnp.int32)
    pos_s = jnp.tile(past + jnp.arange(ds, dtype=jnp.int32), bs)

    nb_all = bp + bs
    pad = (-nb_all) % 16
    c_all = jnp.concatenate([c_prompt, c_sample, jnp.zeros((pad, d), F32)], axis=0)
    mod_all = _ada(c_all, w_ada, b_ada)

    xp, xs = x_prompt, x_sample
    ckv_p_l, kpe_p_l, ckv_s_l, kpe_s_l = [], [], [], []
    pool_p_l, pool_s_l, conv_p_l, conv_s_l = [], [], [], []
    for i in range(depth):
        j = i // 2
        mod_p = mod_all[i, :bp].reshape(bp, 6, 1, d)
        mod_s = mod_all[i, bp:nb_all].reshape(bs, 6, 1, d)
        w_up_b = None
        if i % 2 == 0:
            w_in = jnp.concatenate([w_dq[j], w_dkv[j], jnp.zeros((d, 64), F32)], axis=1).astype(BF16)
            w_q_nope = w_uq[j][:, :, :128].reshape(q_lora, n_heads * 128).astype(BF16)
            w_q_pe = w_uq[j][:, :, 128:].reshape(q_lora, n_heads * 64).astype(BF16)
            w_ukv3 = w_ukv.reshape(w_ukv.shape[0], kv_lora, n_heads * 256)
            cq, ckv_p, ckv_pb, kpe_p, kpe_pb = _mla_project(xp, norm_mix_w, i, mod_p, pos_p, w_in, q_norm_w, kv_norm_w, j,
                                                            q_lora, kv_lora)
            qn, qp = _q_heads(cq, w_q_nope, w_q_pe, pos_p, scale)
            kn, vv = _kv_heads(ckv_pb, w_ukv3, j, n_heads)
            r3 = lambda t: t.reshape(bp, s, t.shape[-1])
            o_p, *w_up_b = _flash(r3(qn), r3(qp), r3(kn), r3(vv), r3(kpe_pb), n_heads, w_up, i)
            xp = _proj_residual("attn_out_prompt", o_p.reshape(bp * s, n_heads * 128), w_o, j, xp, mod_p, 2)
            cq, ckv_s, _, kpe_s, _ = _mla_project(xs, norm_mix_w, i, mod_s, pos_s, w_in, q_norm_w, kv_norm_w, j,
                                                  q_lora, kv_lora)
            qn, qp = _q_heads(cq, w_q_nope, w_q_pe, pos_s, scale)
            q_lat = _absorb_q(qn, w_ukv3, j, n_heads, bs, ds)
            qp_heads = qp.reshape(bs, ds, n_heads, 64).transpose(0, 2, 1, 3)
            o_lat = _sample_attention(q_lat, qp_heads, cache_ckv, jnp.swapaxes(cache_kpe, 2, 3), j, ckv_s.reshape(bs, ds, kv_lora),
                                      kpe_s.reshape(bs, ds, 64))
            o_s = _expand_v(o_lat, w_ukv3, j)
            xs = _proj_residual("attn_out_sample", o_s, w_o, j, xs, mod_s, 2)
            ckv_p_l.append(ckv_p.reshape(bp, s, kv_lora))
            kpe_p_l.append(kpe_p.reshape(bp, s, 64))
            ckv_s_l.append(ckv_s.reshape(bs, ds, kv_lora))
            kpe_s_l.append(kpe_s.reshape(bs, ds, 64))
        else:
            hist = state_pool.shape[2]
            xp, tail_p = _pool_mix_start(xp, norm_mix_w, i, mod_p, w_pool, pool_scale, j)
            us = _normmod(xs, norm_mix_w, i, mod_s, 0, 1, F32)
            pool_p_l.append(tail_p[:, tail_p.shape[1] - hist:])
            pool_s_l.append(jnp.concatenate([state_pool[j], us], axis=1)[:, -hist:])
            xs = _pool_mix(us, state_pool, w_pool, pool_scale, j, xs, mod_s, 2, past)
        hp = _normmod(xp, norm_ffn_w, i, mod_p, 3, 4, BF16)
        hs = _normmod(xs, norm_ffn_w, i, mod_s, 3, 4, BF16)
        if w_up_b is None:
            act_s, cst_s, *w_up_b = _ffn_up_sample(hs, w_up, w_up, conv_w, conv_b, i, state_conv)
        else:
            act_s, cst_s = _ffn_up_sample(hs, *w_up_b, conv_w, conv_b, i, state_conv)
        act_p, cst_p, w_down_b = _ffn_up_prompt(hp, *w_up_b, conv_w, conv_b, w_down, i)
        conv_p_l.append(cst_p)
        conv_s_l.append(cst_s)
        xs = _proj_residual("ffn_down_sample", act_s, w_down_b, None, xs, mod_s, 5)
        xp = _proj_residual("ffn_down_prompt", act_p, w_down_b, None, xp, mod_p, 5)

    y_prompt = _final_norm(xp, final_norm_w)
    y_sample = _final_norm(xs, final_norm_w)
    return (y_prompt, y_sample,
            jnp.stack(ckv_p_l), jnp.stack(kpe_p_l), jnp.stack(ckv_s_l), jnp.stack(kpe_s_l),
            jnp.stack(pool_p_l), jnp.stack(pool_s_l), jnp.stack(conv_p_l), jnp.stack(conv_s_l))
```
